```python
import jax, jax.numpy as jnp
from jax import lax
import numpy as np

D_MODEL = 1024
BATCH = 4
SEQ = 8192
DEPTH = 1
DEC_BATCH = 32
DEC_SEQ = 8
PAST_LEN = 16384
PAGE_SIZE = 128

HEAD_DIM = 128
N_HEADS = D_MODEL // HEAD_DIM
D_ATT = N_HEADS * HEAD_DIM
ATT_SCALE = HEAD_DIM ** -0.5
Q_BLOCK = 128
FORGET_BIAS_INIT = 4.0
D_LRU = D_MODEL
LRU_HEADS = 16
LRU_BLOCK = D_LRU // LRU_HEADS
LRU_CONV_W = 4
LRU_C = 8.0
D_FF = 3 * D_MODEL
FFN_CONV_W = 3
D_PLE = 256
RMS_EPS = 1e-6
IN_SIZES = (D_LRU, D_ATT, D_ATT, D_ATT, N_HEADS, D_LRU, D_ATT)
IN_COLS = sum(IN_SIZES)
IN_SPLITS = tuple(int(s) for s in np.cumsum(IN_SIZES)[:-1])

kernel_name = "hybrid_rglru_fox_convffn_step"


def rms_norm(x, g):
    xf = x.astype(jnp.float32)
    y = xf * lax.rsqrt(jnp.mean(xf * xf, axis=-1, keepdims=True) + RMS_EPS)
    return (y * g.astype(jnp.float32)).astype(x.dtype)


def causal_dwconv(x, buf, w, b):
    width = w.shape[0]
    t = x.shape[1]
    xp = jnp.concatenate([buf.astype(x.dtype), x], axis=1)
    y = xp[:, 0:t] * w[0]
    for j in range(1, width):
        y = y + xp[:, j:j + t] * w[j]
    return y + b, xp[:, xp.shape[1] - (width - 1):]


def _lru_combine(e1, e2):
    a1, b1 = e1
    a2, b2 = e2
    return a1 * a2, a2 * b1 + b2


def rg_lru(x, h0, wa, ba, wx, bx, lam):
    bsz, t, _ = x.shape
    xf = x.astype(jnp.float32)
    xh = xf.reshape(bsz, t, LRU_HEADS, LRU_BLOCK)
    r = jax.nn.sigmoid(jnp.einsum("bthi,hij->bthj", xh, wa.astype(jnp.float32)).reshape(bsz, t, D_LRU) + ba.astype(jnp.float32))
    i = jax.nn.sigmoid(jnp.einsum("bthi,hij->bthj", xh, wx.astype(jnp.float32)).reshape(bsz, t, D_LRU) + bx.astype(jnp.float32))
    log_a = -LRU_C * r * jax.nn.softplus(-lam.astype(jnp.float32))
    a = jnp.exp(log_a)
    bterm = jnp.sqrt(-jnp.expm1(2.0 * log_a)) * (i * xf)
    bterm = bterm.at[:, 0].add(a[:, 0] * h0.astype(jnp.float32))
    _, h = lax.associative_scan(_lru_combine, (a, bterm), axis=1)
    return h.astype(x.dtype), h[:, -1].astype(x.dtype)


def attend_prompt(q, k, v, logf):
    b, s, h, dh = q.shape
    nb = s // Q_BLOCK
    c = lax.cumsum(logf.astype(jnp.float32), axis=1)
    c_k = jnp.swapaxes(c, 1, 2)
    q_blocks = jnp.swapaxes(q.reshape(b, nb, Q_BLOCK, h, dh), 0, 1)
    c_blocks = jnp.swapaxes(c.reshape(b, nb, Q_BLOCK, h), 0, 1)
    starts = jnp.arange(nb, dtype=jnp.int32) * Q_BLOCK
    k_pos = jnp.arange(s, dtype=jnp.int32)

    def one_block(args):
        qb, cb, s0 = args
        logits = jnp.einsum("bqhd,bkhd->bhqk", qb, k).astype(jnp.float32) * ATT_SCALE
        logits = logits + jnp.swapaxes(cb, 1, 2)[..., None] - c_k[:, :, None, :]
        q_pos = s0 + jnp.arange(Q_BLOCK, dtype=jnp.int32)
        logits = jnp.where(q_pos[:, None] >= k_pos[None, :], logits, -jnp.inf)
        probs = jax.nn.softmax(logits, axis=-1).astype(v.dtype)
        return jnp.einsum("bhqk,bkhd->bqhd", probs, v)

    o = lax.map(one_block, (q_blocks, c_blocks, starts))
    return jnp.swapaxes(o, 0, 1).reshape(b, s, h, dh)


def attend_sample(q, k, v, logf, k_past, v_past, logf_past):
    t = q.shape[1]
    n_past = k_past.shape[1]
    lp = logf_past.astype(jnp.float32)
    suffix = lax.cumsum(lp, axis=1, reverse=True) - lp
    c_new = lax.cumsum(logf.astype(jnp.float32), axis=1)
    c_q = jnp.swapaxes(c_new, 1, 2)[..., None]
    l_past = jnp.einsum("bqhd,bkhd->bhqk", q, k_past).astype(jnp.float32) * ATT_SCALE
    l_past = l_past + c_q + jnp.swapaxes(suffix, 1, 2)[:, :, None, :]
    l_new = jnp.einsum("bqhd,bkhd->bhqk", q, k).astype(jnp.float32) * ATT_SCALE
    l_new = l_new + c_q - jnp.swapaxes(c_new, 1, 2)[:, :, None, :]
    causal = jnp.tril(jnp.ones((t, t), dtype=bool))
    l_new = jnp.where(causal, l_new, -jnp.inf)
    probs = jax.nn.softmax(jnp.concatenate([l_past, l_new], axis=-1), axis=-1).astype(v.dtype)
    return (jnp.einsum("bhqk,bkhd->bqhd", probs[..., :n_past], v_past)
            + jnp.einsum("bhqk,bkhd->bqhd", probs[..., n_past:], v))


def decoder_layer(x, p_l, attend, lru_buf, lru_h0, ffn_buf,
                  g_mix_pre, w_in, b_f, lru_cw, lru_cb, lru_wa, lru_ba, lru_wx, lru_bx, lru_lam,
                  w_out, g_mix_post, g_ffn_pre, w_up, ffn_cw, ffn_cb, w_down, g_ffn_post,
                  w_ple, w_ple_gate, g_ple_post):
    bsz, t, _ = x.shape
    u = rms_norm(x, g_mix_pre)
    z = u @ w_in
    x_lru, q, k, v, f_logit, g_lru, g_att = jnp.split(z, IN_SPLITS, axis=-1)
    logf = jax.nn.log_sigmoid(f_logit.astype(jnp.float32) + b_f.astype(jnp.float32))
    xc, lru_buf_new = causal_dwconv(x_lru, lru_buf, lru_cw, lru_cb)
    o_lru, h_last = rg_lru(xc, lru_h0, lru_wa, lru_ba, lru_wx, lru_bx, lru_lam)
    q = q.reshape(bsz, t, N_HEADS, HEAD_DIM)
    k = k.reshape(bsz, t, N_HEADS, HEAD_DIM)
    v = v.reshape(bsz, t, N_HEADS, HEAD_DIM)
    o_att = attend(q, k, v, logf).reshape(bsz, t, D_ATT)
    merged = jax.nn.sigmoid(g_lru) * o_lru + jax.nn.sigmoid(g_att) * o_att
    x = x + rms_norm(merged @ w_out, g_mix_post)
    hu = rms_norm(x, g_ffn_pre) @ w_up
    hc, ffn_buf_new = causal_dwconv(hu, ffn_buf, ffn_cw, ffn_cb)
    gate, val = jnp.split(hc, 2, axis=-1)
    x = x + rms_norm((jax.nn.gelu(gate, approximate=True) * val) @ w_down, g_ffn_post)
    e = (p_l.astype(x.dtype) @ w_ple) * jax.nn.sigmoid(x @ w_ple_gate)
    x = x + rms_norm(e, g_ple_post)
    return x, (k, v, logf, h_last, lru_buf_new, ffn_buf_new)


def setup_inputs(seed: int = 0) -> dict:
    key = jax.random.key(seed)
    keys = iter(jax.random.split(key, 48))
    n_pages = PAST_LEN // PAGE_SIZE
    n_pool = (DEC_BATCH * n_pages * 5) // 4
    f32 = jnp.float32

    def nrm(shape, scale):
        return jax.random.normal(next(keys), shape, f32) * scale

    def gain(shape):
        return 1.0 + nrm(shape, 0.05)

    x_prompt = nrm((BATCH, SEQ, D_MODEL), 1.0)
    x_sample = nrm((DEC_BATCH, DEC_SEQ, D_MODEL), 1.0)
    cache_k = nrm((DEPTH, n_pool, PAGE_SIZE, N_HEADS, HEAD_DIM), 1.0)
    cache_v = nrm((DEPTH, n_pool, PAGE_SIZE, N_HEADS, HEAD_DIM), 1.0)
    cache_logf = jax.nn.log_sigmoid(FORGET_BIAS_INIT + nrm((DEPTH, n_pool, PAGE_SIZE, N_HEADS), 0.5))
    state_lru_h = nrm((DEPTH, DEC_BATCH, D_LRU), 0.5)
    state_lru_conv = nrm((DEPTH, DEC_BATCH, LRU_CONV_W - 1, D_LRU), 1.0)
    state_ffn_conv = nrm((DEPTH, DEC_BATCH, FFN_CONV_W - 1, 2 * D_FF), 1.0)
    page_table = jax.random.permutation(next(keys), n_pool)[:DEC_BATCH * n_pages].reshape(DEC_BATCH, n_pages).astype(jnp.int32)
    p_prompt = nrm((DEPTH, BATCH, SEQ, D_PLE), 1.0)
    p_sample = nrm((DEPTH, DEC_BATCH, DEC_SEQ, D_PLE), 1.0)

    u = jax.random.uniform(next(keys), (DEPTH, D_LRU), f32, minval=0.9, maxval=0.999)
    a0 = u ** (1.0 / LRU_C)
    lru_lambda = jnp.log(a0) - jnp.log1p(-a0)

    return {
        "x_prompt": x_prompt, "x_sample": x_sample,
        "cache_k": cache_k, "cache_v": cache_v, "cache_logf": cache_logf,
        "state_lru_h": state_lru_h, "state_lru_conv": state_lru_conv, "state_ffn_conv": state_ffn_conv,
        "page_table": page_table, "p_prompt": p_prompt, "p_sample": p_sample,
        "norm_mix_pre": gain((DEPTH, D_MODEL)),
        "w_in": nrm((DEPTH, D_MODEL, IN_COLS), D_MODEL ** -0.5),
        "b_forget": FORGET_BIAS_INIT + nrm((DEPTH, N_HEADS), 0.1),
        "lru_conv_w": nrm((DEPTH, LRU_CONV_W, D_LRU), LRU_CONV_W ** -0.5),
        "lru_conv_b": nrm((DEPTH, D_LRU), 0.02),
        "lru_wa": nrm((DEPTH, LRU_HEADS, LRU_BLOCK, LRU_BLOCK), LRU_BLOCK ** -0.5),
        "lru_ba": nrm((DEPTH, D_LRU), 0.02),
        "lru_wx": nrm((DEPTH, LRU_HEADS, LRU_BLOCK, LRU_BLOCK), LRU_BLOCK ** -0.5),
        "lru_bx": nrm((DEPTH, D_LRU), 0.02),
        "lru_lambda": lru_lambda,
        "w_out": nrm((DEPTH, D_MODEL, D_MODEL), D_MODEL ** -0.5),
        "norm_mix_post": gain((DEPTH, D_MODEL)),
        "norm_ffn_pre": gain((DEPTH, D_MODEL)),
        "w_up": nrm((DEPTH, D_MODEL, 2 * D_FF), D_MODEL ** -0.5),
        "ffn_conv_w": nrm((DEPTH, FFN_CONV_W, 2 * D_FF), FFN_CONV_W ** -0.5),
        "ffn_conv_b": nrm((DEPTH, 2 * D_FF), 0.02),
        "w_down": nrm((DEPTH, D_FF, D_MODEL), D_FF ** -0.5),
        "norm_ffn_post": gain((DEPTH, D_MODEL)),
        "w_ple": nrm((DEPTH, D_PLE, D_MODEL), D_PLE ** -0.5),
        "w_ple_gate": nrm((DEPTH, D_MODEL, D_MODEL), D_MODEL ** -0.5),
        "norm_ple_post": gain((DEPTH, D_MODEL)),
    }


def reference(x_prompt, x_sample, cache_k, cache_v, cache_logf, state_lru_h, state_lru_conv, state_ffn_conv,
              page_table, p_prompt, p_sample, norm_mix_pre, w_in, b_forget, lru_conv_w, lru_conv_b,
              lru_wa, lru_ba, lru_wx, lru_bx, lru_lambda, w_out, norm_mix_post, norm_ffn_pre, w_up,
              ffn_conv_w, ffn_conv_b, w_down, norm_ffn_post, w_ple, w_ple_gate, norm_ple_post):
    y_p, y_s = x_prompt, x_sample
    bp = x_prompt.shape[0]
    db = x_sample.shape[0]
    n_past = page_table.shape[1] * cache_k.shape[2]
    kp_l, ks_l, vp_l, vs_l, fp_l, fs_l = [], [], [], [], [], []
    hp_l, hs_l, cp_l, cs_l, mp_l, ms_l = [], [], [], [], [], []
    for l in range(DEPTH):
        lw = (norm_mix_pre[l], w_in[l], b_forget[l], lru_conv_w[l], lru_conv_b[l], lru_wa[l], lru_ba[l],
              lru_wx[l], lru_bx[l], lru_lambda[l], w_out[l], norm_mix_post[l], norm_ffn_pre[l], w_up[l],
              ffn_conv_w[l], ffn_conv_b[l], w_down[l], norm_ffn_post[l], w_ple[l], w_ple_gate[l], norm_ple_post[l])
        y_p, st_p = decoder_layer(
            y_p, p_prompt[l], attend_prompt,
            jnp.zeros((bp, LRU_CONV_W - 1, D_LRU), y_p.dtype), jnp.zeros((bp, D_LRU), y_p.dtype),
            jnp.zeros((bp, FFN_CONV_W - 1, 2 * D_FF), y_p.dtype), *lw)
        k_past = cache_k[l][page_table].reshape(db, n_past, N_HEADS, HEAD_DIM)
        v_past = cache_v[l][page_table].reshape(db, n_past, N_HEADS, HEAD_DIM)
        f_past = cache_logf[l][page_table].reshape(db, n_past, N_HEADS)
        attend = lambda q, k, v, f, kp=k_past, vp=v_past, fp=f_past: attend_sample(q, k, v, f, kp, vp, fp)
        y_s, st_s = decoder_layer(y_s, p_sample[l], attend, state_lru_conv[l], state_lru_h[l], state_ffn_conv[l], *lw)
        kp_l.append(st_p[0]); vp_l.append(st_p[1]); fp_l.append(st_p[2])
        hp_l.append(st_p[3]); cp_l.append(st_p[4]); mp_l.append(st_p[5])
        ks_l.append(st_s[0]); vs_l.append(st_s[1]); fs_l.append(st_s[2])
        hs_l.append(st_s[3]); cs_l.append(st_s[4]); ms_l.append(st_s[5])
    return (y_p, y_s,
            jnp.stack(kp_l), jnp.stack(ks_l), jnp.stack(vp_l), jnp.stack(vs_l),
            jnp.stack(fp_l), jnp.stack(fs_l), jnp.stack(hp_l), jnp.stack(hs_l),
            jnp.stack(cp_l), jnp.stack(cs_l), jnp.stack(mp_l), jnp.stack(ms_l))
```

```python
import functools

import numpy as np
import jax
import jax.numpy as jnp
from jax import lax
from jax.experimental import pallas as pl
from jax.experimental.pallas import tpu as pltpu

D_MODEL = 1024
N_HEADS = 8
HEAD_DIM = 128
ATT_SCALE = HEAD_DIM ** -0.5
LRU_HEADS = 16
LRU_BLOCK = 64
LRU_CONV_W = 4
LRU_C = 8.0
D_FF = 3 * D_MODEL
FFN_CONV_W = 3
RMS_EPS = 1e-6

LANES = 128
SUBLANES = 8
MXU_DIM = 256
AUG_DIM = 2 * HEAD_DIM
VMEM_LIMIT = 56 * 1024 * 1024

ROW_TILE = 256
ATT_TILE = 256
FFN_CHUNK = 512
PAGES_PER_STEP = 4

F32 = jnp.float32
BF16 = jnp.bfloat16


def _round_up(n, m):
    return (n + m - 1) // m * m


def _log2(n):
    assert n > 0 and n & (n - 1) == 0
    return n.bit_length() - 1


def _rms(x, g):
    ms = jnp.mean(x * x, axis=-1, keepdims=True)
    return (x * lax.rsqrt(ms + RMS_EPS)) * g


def _sigmoid(x):
    return 1.0 / (1.0 + jnp.exp(-x))


def _softplus(x):
    return jnp.maximum(x, 0.0) + jnp.log1p(jnp.exp(-jnp.abs(x)))


def _dot(a, b):
    return jnp.dot(a, b, preferred_element_type=F32)


def _dot_nt(a, b):
    return lax.dot_general(a, b, (((1,), (1,)), ((), ())), preferred_element_type=F32)


def _split3(c):
    hi = c.astype(BF16).astype(F32)
    r1 = c - hi
    mid = r1.astype(BF16).astype(F32)
    lo = (r1 - mid).astype(BF16).astype(F32)
    return hi, mid, lo


def _const_spec(shape):
    nd = len(shape)
    return pl.BlockSpec(shape, lambda *_: (0,) * nd, pipeline_mode=pl.Buffered(1))


def _in_proj_kernel(x_ref, g_ref, wl_ref, wq_ref, wk_ref, wv_ref, wf_ref, wgl_ref, wga_ref, bf_ref,
                    pq_ref, pk_ref,
                    xl_ref, qa_ref, ka_ref, kf_ref, vf_ref, vb_ref, lf_ref, c_ref, sgl_ref, sga_ref,
                    carry_ref, *, inter, tiles_per_seq):
    tm = x_ref.shape[0]
    i = pl.program_id(0)
    ub = _rms(x_ref[...], g_ref[...]).astype(BF16)

    xl_ref[...] = _dot(ub, wl_ref[...])
    kf = _dot(ub, wk_ref[...])
    kf_ref[...] = kf
    vf = _dot(ub, wv_ref[...])
    vf_ref[...] = vf
    vb_ref[...] = vf.astype(BF16)
    sgl_ref[...] = _sigmoid(_dot(ub, wgl_ref[...])).astype(BF16)
    sga_ref[...] = _sigmoid(_dot(ub, wga_ref[...])).astype(BF16)
    qs = (_dot(ub, wq_ref[...]) * ATT_SCALE).astype(BF16)
    kb = kf.astype(BF16)

    lf = -_softplus(-(_dot(ub, wf_ref[...]) + bf_ref[...]))
    lf_ref[...] = lf
    row = lax.broadcasted_iota(jnp.int32, (tm, LANES), 0)
    c = lf
    d = inter
    while d < tm:
        c = c + jnp.where(row >= d, pltpu.roll(c, d, 0), 0.0)
        d *= 2
    if tiles_per_seq > 1:
        @pl.when(i % tiles_per_seq == 0)
        def _():
            carry_ref[...] = jnp.zeros_like(carry_ref)
        c = c + carry_ref[0:1, :]
        carry_ref[0:1, :] = c[tm - 1:tm, :]
    c_ref[...] = c

    hi, mid, lo = _split3(c)
    lane = lax.broadcasted_iota(jnp.int32, (tm, LANES), 1)
    cp = jnp.where(lane < N_HEADS, hi,
                   jnp.where(lane < 2 * N_HEADS, pltpu.roll(mid, N_HEADS, 1),
                             jnp.where(lane < 3 * N_HEADS, pltpu.roll(lo, 2 * N_HEADS, 1),
                                       jnp.where(lane == 3 * N_HEADS, 1.0, 0.0)))).astype(BF16)
    qx = _dot(cp, pq_ref[...]).astype(BF16)
    kx = _dot(cp, pk_ref[...]).astype(BF16)
    for h in range(N_HEADS):
        src = slice(h * HEAD_DIM, (h + 1) * HEAD_DIM)
        qa_ref[:, h * AUG_DIM:h * AUG_DIM + HEAD_DIM] = qs[:, src]
        qa_ref[:, h * AUG_DIM + HEAD_DIM:(h + 1) * AUG_DIM] = qx[:, src]
        ka_ref[:, h * AUG_DIM:h * AUG_DIM + HEAD_DIM] = kb[:, src]
        ka_ref[:, h * AUG_DIM + HEAD_DIM:(h + 1) * AUG_DIM] = kx[:, src]


def _bias_scatter_matrices():
    pq = np.zeros((LANES, D_MODEL), np.float32)
    pk = np.zeros((LANES, D_MODEL), np.float32)
    for h in range(N_HEADS):
        base = h * HEAD_DIM
        for part in range(3):
            pq[part * N_HEADS + h, base + part] = 1.0
            pq[3 * N_HEADS, base + 3 + part] = 1.0
            pk[3 * N_HEADS, base + part] = 1.0
            pk[part * N_HEADS + h, base + 3 + part] = -1.0
    return jnp.asarray(pq, BF16), jnp.asarray(pk, BF16)


def _in_proj(x2, inter, seq_rows, wts):
    n = x2.shape[0]
    tm = min(ROW_TILE, seq_rows)
    assert n % tm == 0 and seq_rows % tm == 0
    tiles_per_seq = seq_rows // tm
    assert tiles_per_seq == 1 or inter == 1
    pq, pk = _bias_scatter_matrices()
    row = lambda w: pl.BlockSpec((tm, w), lambda i: (i, 0))
    sq = _const_spec((D_MODEL, D_MODEL))
    out_shape = (
        jax.ShapeDtypeStruct((n, D_MODEL), F32),
        jax.ShapeDtypeStruct((n, N_HEADS * AUG_DIM), BF16),
        jax.ShapeDtypeStruct((n, N_HEADS * AUG_DIM), BF16),
        jax.ShapeDtypeStruct((n, D_MODEL), F32),
        jax.ShapeDtypeStruct((n, D_MODEL), F32),
        jax.ShapeDtypeStruct((n, D_MODEL), BF16),
        jax.ShapeDtypeStruct((n, LANES), F32),
        jax.ShapeDtypeStruct((n, LANES), F32),
        jax.ShapeDtypeStruct((n, D_MODEL), BF16),
        jax.ShapeDtypeStruct((n, D_MODEL), BF16),
    )
    return pl.pallas_call(
        functools.partial(_in_proj_kernel, inter=inter, tiles_per_seq=tiles_per_seq),
        grid=(n // tm,),
        in_specs=[row(D_MODEL), _const_spec((1, D_MODEL)), sq, sq, sq, sq,
                  _const_spec((D_MODEL, LANES)), sq, sq, _const_spec((1, LANES)),
                  _const_spec((LANES, D_MODEL)), _const_spec((LANES, D_MODEL))],
        out_specs=(row(D_MODEL), row(N_HEADS * AUG_DIM), row(N_HEADS * AUG_DIM), row(D_MODEL), row(D_MODEL),
                   row(D_MODEL), row(LANES), row(LANES), row(D_MODEL), row(D_MODEL)),
        out_shape=out_shape,
        scratch_shapes=[pltpu.VMEM((SUBLANES, LANES), F32)],
        compiler_params=pltpu.CompilerParams(dimension_semantics=("arbitrary",), vmem_limit_bytes=VMEM_LIMIT),
        name="in_proj",
    )(x2, wts["g_mix_pre"], wts["w_lru"], wts["w_q"], wts["w_k"], wts["w_v"], wts["w_f"], wts["w_gl"],
      wts["w_ga"], wts["b_f"], pq, pk)


def _scan8(a, b):
    row = lax.broadcasted_iota(jnp.int32, a.shape, 0)
    for d in (1, 2, 4):
        keep = row >= d
        a_s = pltpu.roll(a, d, 0)
        b_s = pltpu.roll(b, d, 0)
        b = jnp.where(keep, a * b_s + b, b)
        a = jnp.where(keep, a * a_s, a)
    return a, b


def _lru_kernel(xl_ref, sgl_ref, hist_ref, h0_ref, cw_ref, cb_ref, wab_ref, ba_ref, bx_ref, lam_ref,
                og_ref, hl_ref,
                xp_ref, a_ref, b_ref, hcar_ref, *, inter, tiles_per_seq, steps):
    tt = xl_ref.shape[0]
    nh = (LRU_CONV_W - 1) * inter
    hp = xp_ref.shape[0] - tt
    i = pl.program_id(0)

    @pl.when(i % tiles_per_seq == 0)
    def _():
        xp_ref[hp - nh:hp, :] = hist_ref[0]
        hcar_ref[0:inter, :] = h0_ref[0]

    xp_ref[hp:hp + tt, :] = xl_ref[...]
    xc = cb_ref[...]
    for j in range(LRU_CONV_W):
        off = hp - (LRU_CONV_W - 1 - j) * inter
        xc = xc + cw_ref[j:j + 1, :] * xp_ref[off:off + tt, :]
    xp_ref[hp - nh:hp, :] = xp_ref[hp + tt - nh:hp + tt, :]

    xcb = xc.astype(BF16)
    sp = _softplus(-lam_ref[...])
    for blk in range(D_MODEL // MXU_DIM):
        cols = slice(blk * MXU_DIM, (blk + 1) * MXU_DIM)
        gates = _dot(xcb[:, cols], wab_ref[blk])
        r = _sigmoid(gates[:, :MXU_DIM] + ba_ref[:, cols])
        ig = _sigmoid(gates[:, MXU_DIM:] + bx_ref[:, cols])
        log_a = (-LRU_C * r) * sp[:, cols]
        a_ref[:, cols] = jnp.exp(log_a)
        th = jnp.tanh(log_a)
        b_ref[:, cols] = jnp.sqrt((-2.0 * th) / (1.0 - th)) * (ig * xc[:, cols])

    if inter == 1:
        def body(g, carry):
            r0 = pl.multiple_of(g * SUBLANES, SUBLANES)
            a8, b8 = _scan8(a_ref[pl.ds(r0, SUBLANES), :], b_ref[pl.ds(r0, SUBLANES), :])
            h8 = a8 * carry + b8
            b_ref[pl.ds(r0, SUBLANES), :] = h8
            return jnp.broadcast_to(h8[SUBLANES - 1:SUBLANES, :], (SUBLANES, D_MODEL))

        h_fin = lax.fori_loop(0, tt // SUBLANES, body,
                              jnp.broadcast_to(hcar_ref[0:1, :], (SUBLANES, D_MODEL)))
        hcar_ref[0:1, :] = h_fin[0:1, :]
    else:
        h = hcar_ref[0:inter, :]
        for t in range(steps):
            rows = slice(t * inter, (t + 1) * inter)
            h = a_ref[rows, :] * h + b_ref[rows, :]
            b_ref[rows, :] = h
        hcar_ref[0:inter, :] = h
    og_ref[...] = (sgl_ref[...].astype(F32) * b_ref[...]).astype(BF16)
    hl_ref[0] = hcar_ref[0:inter, :]


def _lru(xl, sgl, hist, h0, inter, seq_rows, wts):
    n = xl.shape[0]
    tt = min(ROW_TILE, seq_rows)
    tiles_per_seq = seq_rows // tt
    assert tiles_per_seq == 1 or inter == 1
    n_groups = n // seq_rows
    nh = (LRU_CONV_W - 1) * inter
    assert tt >= nh
    hp = _round_up(nh, SUBLANES)
    row = lambda w: pl.BlockSpec((tt, w), lambda i: (i, 0))
    grp = lambda r: pl.BlockSpec((1, r, D_MODEL), lambda i: (i // tiles_per_seq, 0, 0))
    vec = _const_spec((1, D_MODEL))
    return pl.pallas_call(
        functools.partial(_lru_kernel, inter=inter, tiles_per_seq=tiles_per_seq, steps=seq_rows // inter),
        grid=(n // tt,),
        in_specs=[row(D_MODEL), row(D_MODEL), grp(nh), grp(inter), _const_spec((LRU_CONV_W, D_MODEL)), vec,
                  _const_spec((D_MODEL // MXU_DIM, MXU_DIM, 2 * MXU_DIM)), vec, vec, vec],
        out_specs=(row(D_MODEL), grp(inter)),
        out_shape=(jax.ShapeDtypeStruct((n, D_MODEL), BF16),
                   jax.ShapeDtypeStruct((n_groups, inter, D_MODEL), F32)),
        scratch_shapes=[pltpu.VMEM((hp + tt, D_MODEL), F32), pltpu.VMEM((tt, D_MODEL), F32),
                        pltpu.VMEM((tt, D_MODEL), F32), pltpu.VMEM((max(inter, SUBLANES), D_MODEL), F32)],
        compiler_params=pltpu.CompilerParams(dimension_semantics=("arbitrary",), vmem_limit_bytes=VMEM_LIMIT),
        name="rg_lru",
    )(xl, sgl, hist, h0, wts["lru_cw"], wts["lru_cb"], wts["lru_wab"], wts["lru_ba"], wts["lru_bx"],
      wts["lru_lam"])


def _attn_prompt_kernel(q_ref, k_ref, v_ref, o_ref):
    tq = q_ref.shape[1]
    i = pl.program_id(2)
    q = q_ref[0]

    def step(j, carry, masked):
        m, l, acc = carry
        k0 = pl.multiple_of(j * tq, tq)
        s = _dot_nt(q, k_ref[0, pl.ds(k0, tq), :])
        if masked:
            r = lax.broadcasted_iota(jnp.int32, (tq, tq), 0)
            c = lax.broadcasted_iota(jnp.int32, (tq, tq), 1)
            s = jnp.where(c <= r, s, -jnp.inf)
        m_new = jnp.maximum(m, jnp.max(s, axis=-1, keepdims=True))
        alpha = jnp.exp(m - m_new)
        p = jnp.exp(s - m_new)
        l = alpha * l + jnp.sum(p, axis=-1, keepdims=True)
        acc = alpha * acc + _dot(p.astype(BF16), v_ref[0, pl.ds(k0, tq), :])
        return m_new, l, acc

    init = (jnp.full((tq, 1), -jnp.inf, F32), jnp.zeros((tq, 1), F32), jnp.zeros((tq, HEAD_DIM), F32))
    carry = lax.fori_loop(0, i, lambda j, c: step(j, c, False), init)
    _, l, acc = step(i, carry, True)
    o_ref[0] = (acc / l).astype(o_ref.dtype)


def _attn_prompt(qa, ka, vb):
    b, s, _ = qa.shape
    tq = min(ATT_TILE, s)
    assert s % tq == 0
    return pl.pallas_call(
        _attn_prompt_kernel,
        grid=(b, N_HEADS, s // tq),
        in_specs=[pl.BlockSpec((1, tq, AUG_DIM), lambda b_, h, i: (b_, i, h)),
                  pl.BlockSpec((1, s, AUG_DIM), lambda b_, h, i: (b_, 0, h)),
                  pl.BlockSpec((1, s, HEAD_DIM), lambda b_, h, i: (b_, 0, h))],
        out_specs=pl.BlockSpec((1, tq, HEAD_DIM), lambda b_, h, i: (b_, i, h)),
        out_shape=jax.ShapeDtypeStruct((b, s, D_MODEL), BF16),
        compiler_params=pltpu.CompilerParams(dimension_semantics=("arbitrary", "arbitrary", "arbitrary"),
                                             vmem_limit_bytes=VMEM_LIMIT),
        name="attn_prompt",
    )(qa, ka, vb)


def _attn_sample_kernel(pt_ref, *refs, n_pb):
    k_refs = refs[0:n_pb]
    v_refs = refs[n_pb:2 * n_pb]
    f_refs = refs[2 * n_pb:3 * n_pb]
    q_ref, qa_ref, ka_ref, vn_ref, cq_ref, u_ref, o_ref, qb_ref, m_ref, l_ref, acc_ref, tail_ref = refs[3 * n_pb:]
    del pt_ref
    t_new = q_ref.shape[1]
    nq = N_HEADS * t_new
    p_idx = pl.program_id(1)

    def head_mask(width):
        r = lax.broadcasted_iota(jnp.int32, (nq, N_HEADS * width), 0)
        c = lax.broadcasted_iota(jnp.int32, (nq, N_HEADS * width), 1)
        return (r >> _log2(t_new)) == (c >> _log2(width))

    def per_head_rows(x):
        return jnp.concatenate([x] * N_HEADS, axis=0)

    @pl.when(p_idx == 0)
    def _():
        qb_ref[...] = jnp.where(head_mask(HEAD_DIM), per_head_rows(q_ref[0]), 0.0).astype(BF16)
        qab = jnp.where(head_mask(AUG_DIM), per_head_rows(qa_ref[0]), 0.0).astype(BF16)
        pad = LANES - t_new
        kn = jnp.concatenate([ka_ref[0], jnp.zeros((pad, N_HEADS * AUG_DIM), F32)], axis=0).astype(BF16)
        vn = jnp.concatenate([vn_ref[0], jnp.zeros((pad, D_MODEL), F32)], axis=0).astype(BF16)
        s = _dot_nt(qab, kn)
        r = lax.broadcasted_iota(jnp.int32, (nq, LANES), 0)
        c = lax.broadcasted_iota(jnp.int32, (nq, LANES), 1)
        s = jnp.where(c <= (r & (t_new - 1)), s, -jnp.inf)
        m = jnp.max(s, axis=-1, keepdims=True)
        p = jnp.exp(s - m)
        m_ref[...] = m
        l_ref[...] = jnp.sum(p, axis=-1, keepdims=True)
        acc_ref[...] = _dot(p.astype(BF16), vn)
        tail_ref[...] = jnp.zeros_like(tail_ref)

    qb = qb_ref[...]
    cq = cq_ref[0]
    tail = tail_ref[...]
    zeros8 = jnp.zeros((SUBLANES, LANES), F32)
    s_parts = []
    for u in range(n_pb):
        lp = f_refs[u][0]
        hi, mid, lo = _split3(lp)
        parts = _dot(jnp.concatenate([hi, mid, lo, zeros8], axis=0).astype(BF16), u_ref[...])
        suf = (parts[0:N_HEADS] + parts[N_HEADS:2 * N_HEADS] + parts[2 * N_HEADS:3 * N_HEADS]) + tail
        tail = tail + jnp.sum(lp, axis=-1, keepdims=True)
        bias = jnp.concatenate([jnp.broadcast_to(suf[h:h + 1, :], (t_new, LANES)) for h in range(N_HEADS)], axis=0)
        s_parts.append(_dot_nt(qb, k_refs[u][0].astype(BF16)) + bias + cq)
    tail_ref[...] = tail
    s = jnp.concatenate(s_parts, axis=1)
    m_old = m_ref[...]
    m_new = jnp.maximum(m_old, jnp.max(s, axis=-1, keepdims=True))
    alpha = jnp.exp(m_old - m_new)
    p = jnp.exp(s - m_new)
    m_ref[...] = m_new
    l_ref[...] = alpha * l_ref[...] + jnp.sum(p, axis=-1, keepdims=True)
    pv = _dot(p[:, 0:LANES].astype(BF16), v_refs[0][0].astype(BF16))
    for u in range(1, n_pb):
        pv = pv + _dot(p[:, u * LANES:(u + 1) * LANES].astype(BF16), v_refs[u][0].astype(BF16))
    acc_ref[...] = alpha * acc_ref[...] + pv

    @pl.when(p_idx == pl.num_programs(1) - 1)
    def _():
        for h in range(N_HEADS):
            rows = slice(h * t_new, (h + 1) * t_new)
            cols = slice(h * HEAD_DIM, (h + 1) * HEAD_DIM)
            o_ref[0, :, cols] = (acc_ref[rows, cols] / l_ref[rows, :]).astype(o_ref.dtype)


def _attn_sample(page_table, cache_k, cache_v, cache_logf_t, q, qa, ka, vn, cq):
    db, n_pages = page_table.shape
    n_pool, page = cache_k.shape[0], cache_k.shape[1]
    assert page == LANES
    t_new = q.shape[1]
    assert t_new == SUBLANES
    n_pb = PAGES_PER_STEP if n_pages % PAGES_PER_STEP == 0 else 1
    nq = N_HEADS * t_new
    u_mat = jnp.asarray(np.tril(np.ones((LANES, LANES), np.float32), -1), BF16)

    def page_map(u):
        def index_map(b, p, pt):
            return (pt[b * n_pages + (n_pages - 1 - (p * n_pb + u))], 0, 0)
        return index_map

    per_b = lambda r, w: pl.BlockSpec((1, r, w), lambda b, p, pt: (b, 0, 0))
    in_specs = ([pl.BlockSpec((1, page, D_MODEL), page_map(u)) for u in range(n_pb)]
                + [pl.BlockSpec((1, page, D_MODEL), page_map(u)) for u in range(n_pb)]
                + [pl.BlockSpec((1, N_HEADS, page), page_map(u)) for u in range(n_pb)]
                + [per_b(t_new, D_MODEL), per_b(t_new, N_HEADS * AUG_DIM), per_b(t_new, N_HEADS * AUG_DIM),
                   per_b(t_new, D_MODEL), per_b(nq, 1),
                   pl.BlockSpec((LANES, LANES), lambda b, p, pt: (0, 0))])
    grid_spec = pltpu.PrefetchScalarGridSpec(
        num_scalar_prefetch=1,
        grid=(db, n_pages // n_pb),
        in_specs=in_specs,
        out_specs=pl.BlockSpec((1, t_new, D_MODEL), lambda b, p, pt: (b, 0, 0)),
        scratch_shapes=[pltpu.VMEM((nq, D_MODEL), BF16), pltpu.VMEM((nq, 1), F32), pltpu.VMEM((nq, 1), F32),
                        pltpu.VMEM((nq, D_MODEL), F32), pltpu.VMEM((N_HEADS, 1), F32)],
    )
    return pl.pallas_call(
        functools.partial(_attn_sample_kernel, n_pb=n_pb),
        grid_spec=grid_spec,
        out_shape=jax.ShapeDtypeStruct((db, t_new, D_MODEL), F32),
        compiler_params=pltpu.CompilerParams(dimension_semantics=("arbitrary", "arbitrary"),
                                             vmem_limit_bytes=VMEM_LIMIT),
        name="attn_sample",
    )(page_table.reshape(-1), *([cache_k] * n_pb), *([cache_v] * n_pb), *([cache_logf_t] * n_pb),
      q, qa, ka, vn, cq, u_mat)


def _gelu_tanh(x):
    return 0.5 * x * (1.0 + jnp.tanh(np.float32(np.sqrt(2.0 / np.pi)) * (x + 0.044715 * (x * x * x))))


def _post_kernel(x_ref, og_ref, oa_ref, sga_ref, p_ref, hist_ref,
                 wo_ref, gpost_ref, gpre_ref, wup_ref, fcw_ref, fcb_ref, wdn_ref, gfpost_ref,
                 wple_ref, wpg_ref, gple_ref,
                 y_ref, hout_ref,
                 hh_ref, xg_ref, xv_ref, *, inter, tiles_per_seq):
    tm = x_ref.shape[0]
    nh = (FFN_CONV_W - 1) * inter
    hp = hh_ref.shape[0]
    i = pl.program_id(0)

    @pl.when(i % tiles_per_seq == 0)
    def _():
        if hp > nh:
            hh_ref[...] = jnp.zeros_like(hh_ref)
        hh_ref[hp - nh:hp, :] = hist_ref[0]

    merged = og_ref[...].astype(F32) + sga_ref[...].astype(F32) * oa_ref[...].astype(F32)
    x1 = x_ref[...] + _rms(_dot(merged.astype(BF16), wo_ref[...]), gpost_ref[...])
    hn = _rms(x1, gpre_ref[...]).astype(BF16)

    y2 = jnp.zeros((tm, D_MODEL), F32)
    for c in range(D_FF // FFN_CHUNK):
        halves = []
        for xp_ref, base in ((xg_ref, 0), (xv_ref, D_FF)):
            cols = slice(base + c * FFN_CHUNK, base + (c + 1) * FFN_CHUNK)
            xp_ref[0:hp, :] = hh_ref[:, cols]
            xp_ref[hp:hp + tm, :] = _dot(hn, wup_ref[:, cols])
            hc = fcb_ref[:, cols]
            for j in range(FFN_CONV_W):
                off = hp - (FFN_CONV_W - 1 - j) * inter
                hc = hc + fcw_ref[j:j + 1, cols] * xp_ref[off:off + tm, :]
            hh_ref[:, cols] = xp_ref[tm:tm + hp, :]
            halves.append(hc)
        act = (_gelu_tanh(halves[0]) * halves[1]).astype(BF16)
        y2 = y2 + _dot(act, wdn_ref[c * FFN_CHUNK:(c + 1) * FFN_CHUNK, :])
    hout_ref[0] = hh_ref[hp - nh:hp, :]

    x2 = x1 + _rms(y2, gfpost_ref[...])
    e = _dot(p_ref[...].astype(BF16), wple_ref[...]) * _sigmoid(_dot(x2.astype(BF16), wpg_ref[...]))
    y_ref[...] = x2 + _rms(e, gple_ref[...])


def _post(x2, og, oa, sga, p2, hist, inter, seq_rows, wts):
    n = x2.shape[0]
    tm = min(ROW_TILE, seq_rows)
    tiles_per_seq = seq_rows // tm
    assert tiles_per_seq == 1 or inter == 1
    n_groups = n // seq_rows
    nh = (FFN_CONV_W - 1) * inter
    hp = _round_up(nh, SUBLANES)
    assert tm >= hp
    d_ple = p2.shape[1]
    row = lambda w: pl.BlockSpec((tm, w), lambda i: (i, 0))
    grp = pl.BlockSpec((1, nh, 2 * D_FF), lambda i: (i // tiles_per_seq, 0, 0))
    vec = _const_spec((1, D_MODEL))
    return pl.pallas_call(
        functools.partial(_post_kernel, inter=inter, tiles_per_seq=tiles_per_seq),
        grid=(n // tm,),
        in_specs=[row(D_MODEL), row(D_MODEL), row(D_MODEL), row(D_MODEL), row(d_ple), grp,
                  _const_spec((D_MODEL, D_MODEL)), vec, vec, _const_spec((D_MODEL, 2 * D_FF)),
                  _const_spec((FFN_CONV_W, 2 * D_FF)), _const_spec((1, 2 * D_FF)), _const_spec((D_FF, D_MODEL)), vec,
                  _const_spec((d_ple, D_MODEL)), _const_spec((D_MODEL, D_MODEL)), vec],
        out_specs=(row(D_MODEL), grp),
        out_shape=(jax.ShapeDtypeStruct((n, D_MODEL), F32),
                   jax.ShapeDtypeStruct((n_groups, nh, 2 * D_FF), F32)),
        scratch_shapes=[pltpu.VMEM((hp, 2 * D_FF), F32), pltpu.VMEM((hp + tm, FFN_CHUNK), F32),
                        pltpu.VMEM((hp + tm, FFN_CHUNK), F32)],
        compiler_params=pltpu.CompilerParams(dimension_semantics=("arbitrary",), vmem_limit_bytes=VMEM_LIMIT),
        name="post_ffn",
    )(x2, og, oa, sga, p2, hist, wts["w_out"], wts["g_mix_post"], wts["g_ffn_pre"], wts["w_up"], wts["ffn_cw"],
      wts["ffn_cb"], wts["w_down"], wts["g_ffn_post"], wts["w_ple"], wts["w_ple_gate"], wts["g_ple_post"])


def _prep_weights(l, norm_mix_pre, w_in, b_forget, lru_conv_w, lru_conv_b, lru_wa, lru_ba, lru_wx, lru_bx,
                  lru_lambda, w_out, norm_mix_post, norm_ffn_pre, w_up, ffn_conv_w, ffn_conv_b, w_down,
                  norm_ffn_post, w_ple, w_ple_gate, norm_ple_post):
    d = D_MODEL
    wi = w_in[l]
    offs = np.cumsum([0, d, d, d, d, N_HEADS, d, d])
    piece = lambda k: wi[:, int(offs[k]):int(offs[k + 1])].astype(BF16)
    vec = lambda a: a[l].reshape(1, -1).astype(F32)
    per_tile = MXU_DIM // LRU_BLOCK
    n_tiles = LRU_HEADS // per_tile

    def block_diag(w):
        w4 = w.reshape(n_tiles, per_tile, LRU_BLOCK, LRU_BLOCK)
        eye = jnp.eye(per_tile, dtype=w.dtype)
        return jnp.einsum("thij,hg->thigj", w4, eye).reshape(n_tiles, MXU_DIM, MXU_DIM)

    return {
        "g_mix_pre": vec(norm_mix_pre),
        "w_lru": piece(0), "w_q": piece(1), "w_k": piece(2), "w_v": piece(3),
        "w_f": jnp.pad(piece(4), ((0, 0), (0, LANES - N_HEADS))),
        "w_gl": piece(5), "w_ga": piece(6),
        "b_f": jnp.pad(vec(b_forget), ((0, 0), (0, LANES - N_HEADS))),
        "lru_cw": lru_conv_w[l].astype(F32), "lru_cb": vec(lru_conv_b),
        "lru_wab": jnp.concatenate([block_diag(lru_wa[l]), block_diag(lru_wx[l])], axis=-1).astype(BF16),
        "lru_ba": vec(lru_ba), "lru_bx": vec(lru_bx), "lru_lam": vec(lru_lambda),
        "w_out": w_out[l].astype(BF16), "g_mix_post": vec(norm_mix_post), "g_ffn_pre": vec(norm_ffn_pre),
        "w_up": w_up[l].astype(BF16), "ffn_cw": ffn_conv_w[l].astype(F32), "ffn_cb": vec(ffn_conv_b),
        "w_down": w_down[l].astype(BF16), "g_ffn_post": vec(norm_ffn_post),
        "w_ple": w_ple[l].astype(BF16), "w_ple_gate": w_ple_gate[l].astype(BF16), "g_ple_post": vec(norm_ple_post),
    }


def _tail_rows(buf, new, width):
    return jnp.concatenate([buf.astype(new.dtype), new], axis=1)[:, -width:]


def _prompt_layer(x, p_l, wts):
    b, s, d = x.shape
    n = b * s
    xl, qa, ka, kf, vf, vb, lf, _, sgl, sga = _in_proj(x.reshape(n, d), 1, s, wts)
    og, h_last = _lru(xl, sgl, jnp.zeros((b, LRU_CONV_W - 1, d), F32), jnp.zeros((b, 1, d), F32), 1, s, wts)
    oa = _attn_prompt(qa.reshape(b, s, -1), ka.reshape(b, s, -1), vb.reshape(b, s, d))
    y, ffn_hist = _post(x.reshape(n, d), og, oa.reshape(n, d), sga, p_l.reshape(n, -1),
                        jnp.zeros((b, FFN_CONV_W - 1, 2 * D_FF), F32), 1, s, wts)
    state = (kf.reshape(b, s, N_HEADS, HEAD_DIM), vf.reshape(b, s, N_HEADS, HEAD_DIM),
             lf[:, :N_HEADS].reshape(b, s, N_HEADS), h_last.reshape(b, d),
             _tail_rows(jnp.zeros((b, LRU_CONV_W - 1, d), F32), xl.reshape(b, s, d), LRU_CONV_W - 1),
             ffn_hist)
    return y.reshape(b, s, d), state


def _sample_layer(x, p_l, cache_k, cache_v, cache_logf, page_table, lru_buf, lru_h0, ffn_buf, wts):
    db, t, d = x.shape
    n = db * t
    tmaj = lambda a: jnp.swapaxes(a, 0, 1).reshape(a.shape[0] * a.shape[1], -1)
    bmaj = lambda a: jnp.swapaxes(a.reshape(-1, db, a.shape[-1]), 0, 1)

    xt = tmaj(x)
    xl, qa, ka, kf, vf, _, lf, c, sgl, sga = _in_proj(xt, db, n, wts)
    og, h_last = _lru(xl, sgl, tmaj(lru_buf)[None], lru_h0[None], db, n, wts)

    qa_b = bmaj(qa).astype(F32)
    q_b = qa_b.reshape(db, t, N_HEADS, AUG_DIM)[..., :HEAD_DIM].reshape(db, t, d)
    cq = jnp.swapaxes(bmaj(c)[..., :N_HEADS], 1, 2).reshape(db, N_HEADS * t, 1)
    n_pool = cache_k.shape[0]
    oa_b = _attn_sample(page_table, cache_k.reshape(n_pool, -1, d), cache_v.reshape(n_pool, -1, d),
                        jnp.swapaxes(cache_logf, 1, 2), q_b, qa_b, bmaj(ka).astype(F32), bmaj(vf), cq)

    y, ffn_hist = _post(xt, og, tmaj(oa_b), sga, tmaj(p_l), tmaj(ffn_buf)[None], db, n, wts)
    k_b, v_b, xl_b = bmaj(kf), bmaj(vf), bmaj(xl)
    state = (k_b.reshape(db, t, N_HEADS, HEAD_DIM), v_b.reshape(db, t, N_HEADS, HEAD_DIM),
             bmaj(lf)[..., :N_HEADS], h_last.reshape(db, d),
             _tail_rows(lru_buf, xl_b, LRU_CONV_W - 1),
             bmaj(ffn_hist.reshape(-1, 2 * D_FF)))
    return bmaj(y), state


def kernel(x_prompt, x_sample, cache_k, cache_v, cache_logf, state_lru_h, state_lru_conv, state_ffn_conv, page_table, p_prompt, p_sample, norm_mix_pre, w_in, b_forget, lru_conv_w, lru_conv_b, lru_wa, lru_ba, lru_wx, lru_bx, lru_lambda, w_out, norm_mix_post, norm_ffn_pre, w_up, ffn_conv_w, ffn_conv_b, w_down, norm_ffn_post, w_ple, w_ple_gate, norm_ple_post):
    depth = w_in.shape[0]
    y_p, y_s = x_prompt, x_sample
    outs_p, outs_s = [], []
    for l in range(depth):
        wts = _prep_weights(l, norm_mix_pre, w_in, b_forget, lru_conv_w, lru_conv_b, lru_wa, lru_ba, lru_wx, lru_bx,
                            lru_lambda, w_out, norm_mix_post, norm_ffn_pre, w_up, ffn_conv_w, ffn_conv_b, w_down,
                            norm_ffn_post, w_ple, w_ple_gate, norm_ple_post)
        y_p, st_p = _prompt_layer(y_p, p_prompt[l], wts)
        y_s, st_s = _sample_layer(y_s, p_sample[l], cache_k[l], cache_v[l], cache_logf[l], page_table,
                                  state_lru_conv[l], state_lru_h[l], state_ffn_conv[l], wts)
        outs_p.append(st_p)
        outs_s.append(st_s)
    stack = lambda outs, k: jnp.stack([o[k] for o in outs])
    res = [y_p, y_s]
    for k in range(6):
        res.append(stack(outs_p, k))
        res.append(stack(outs_s, k))
    return tuple(res)
```

```python
import functools

import numpy as np
import jax
import jax.numpy as jnp
from jax import lax
from jax.experimental import pallas as pl
from jax.experimental.pallas import tpu as pltpu

D_MODEL = 1024
N_HEADS = 8
HEAD_DIM = 128
ATT_SCALE = HEAD_DIM ** -0.5
LRU_HEADS = 16
LRU_BLOCK = 64
LRU_CONV_W = 4
LRU_C = 8.0
D_FF = 3 * D_MODEL
FFN_CONV_W = 3
RMS_EPS = 1e-6

LANES = 128
SUBLANES = 8
MXU_DIM = 256
AUG_DIM = 2 * HEAD_DIM
VMEM_LIMIT = 56 * 1024 * 1024

ROW_TILE = 256
ATT_TILE = 512
ATT_KEY_CHUNK = MXU_DIM
FFN_CHUNK = 512
PAGES_PER_STEP = 8
LOG2E = float(np.log2(np.e))

F32 = jnp.float32
BF16 = jnp.bfloat16


def _round_up(n, m):
    return (n + m - 1) // m * m


def _log2(n):
    assert n > 0 and n & (n - 1) == 0
    return n.bit_length() - 1


def _rms(x, g):
    ms = jnp.mean(x * x, axis=-1, keepdims=True)
    return (x * lax.rsqrt(ms + RMS_EPS)) * g


def _sigmoid(x):
    return 1.0 / (1.0 + jnp.exp(-x))


def _softplus(x):
    return jnp.maximum(x, 0.0) + jnp.log1p(jnp.exp(-jnp.abs(x)))


def _dot(a, b):
    return jnp.dot(a, b, preferred_element_type=F32)


def _dot_nt(a, b):
    return lax.dot_general(a, b, (((1,), (1,)), ((), ())), preferred_element_type=F32)


def _split3(c):
    hi = c.astype(BF16).astype(F32)
    r1 = c - hi
    mid = r1.astype(BF16).astype(F32)
    lo = (r1 - mid).astype(BF16).astype(F32)
    return hi, mid, lo


def _const_spec(shape):
    nd = len(shape)
    return pl.BlockSpec(shape, lambda *_: (0,) * nd, pipeline_mode=pl.Buffered(1))


def _in_proj_kernel(x_ref, g_ref, wl_ref, wq_ref, wk_ref, wv_ref, wf_ref, wgl_ref, wga_ref, bf_ref,
                    pq_ref, pk_ref,
                    xl_ref, qa_ref, ka_ref, kf_ref, vf_ref, vt_ref, lf_ref, c_ref, sgl_ref, sga_ref,
                    carry_ref, *, inter, tiles_per_seq):
    tm = x_ref.shape[0]
    i = pl.program_id(0)
    ub = _rms(x_ref[...], g_ref[...]).astype(BF16)

    xl_ref[...] = _dot(ub, wl_ref[...])
    kf = _dot(ub, wk_ref[...])
    kf_ref[...] = kf
    vf = _dot(ub, wv_ref[...])
    vf_ref[...] = vf
    vt_ref[0] = vf.T.astype(BF16)
    sgl_ref[...] = _sigmoid(_dot(ub, wgl_ref[...])).astype(BF16)
    sga_ref[...] = _sigmoid(_dot(ub, wga_ref[...])).astype(BF16)
    qs = (_dot(ub, wq_ref[...]) * (ATT_SCALE * LOG2E)).astype(BF16)
    kb = kf.astype(BF16)

    lf = -_softplus(-(_dot(ub, wf_ref[...]) + bf_ref[...]))
    lf_ref[...] = lf
    row = lax.broadcasted_iota(jnp.int32, (tm, LANES), 0)
    c = lf
    d = inter
    while d < tm:
        c = c + jnp.where(row >= d, pltpu.roll(c, d, 0), 0.0)
        d *= 2
    if tiles_per_seq > 1:
        @pl.when(i % tiles_per_seq == 0)
        def _():
            carry_ref[...] = jnp.zeros_like(carry_ref)
        c = c + carry_ref[0:1, :]
        carry_ref[0:1, :] = c[tm - 1:tm, :]
    c_ref[...] = c

    hi, mid, lo = _split3(c * LOG2E)
    lane = lax.broadcasted_iota(jnp.int32, (tm, LANES), 1)
    cp = jnp.where(lane < N_HEADS, hi,
                   jnp.where(lane < 2 * N_HEADS, pltpu.roll(mid, N_HEADS, 1),
                             jnp.where(lane < 3 * N_HEADS, pltpu.roll(lo, 2 * N_HEADS, 1),
                                       jnp.where(lane == 3 * N_HEADS, 1.0, 0.0)))).astype(BF16)
    qx = _dot(cp, pq_ref[...]).astype(BF16)
    kx = _dot(cp, pk_ref[...]).astype(BF16)
    for h in range(N_HEADS):
        src = slice(h * HEAD_DIM, (h + 1) * HEAD_DIM)
        qa_ref[:, h * AUG_DIM:h * AUG_DIM + HEAD_DIM] = qs[:, src]
        qa_ref[:, h * AUG_DIM + HEAD_DIM:(h + 1) * AUG_DIM] = qx[:, src]
        ka_ref[:, h * AUG_DIM:h * AUG_DIM + HEAD_DIM] = kb[:, src]
        ka_ref[:, h * AUG_DIM + HEAD_DIM:(h + 1) * AUG_DIM] = kx[:, src]


def _bias_scatter_matrices():
    pq = np.zeros((LANES, D_MODEL), np.float32)
    pk = np.zeros((LANES, D_MODEL), np.float32)
    for h in range(N_HEADS):
        base = h * HEAD_DIM
        for part in range(3):
            pq[part * N_HEADS + h, base + part] = 1.0
            pq[3 * N_HEADS, base + 3 + part] = 1.0
            pk[3 * N_HEADS, base + part] = 1.0
            pk[part * N_HEADS + h, base + 3 + part] = -1.0
    return jnp.asarray(pq, BF16), jnp.asarray(pk, BF16)


def _in_proj(x2, inter, seq_rows, wts):
    n = x2.shape[0]
    tm = min(ROW_TILE, seq_rows)
    assert n % tm == 0 and seq_rows % tm == 0
    tiles_per_seq = seq_rows // tm
    assert tiles_per_seq == 1 or inter == 1
    pq, pk = _bias_scatter_matrices()
    row = lambda w: pl.BlockSpec((tm, w), lambda i: (i, 0))
    sq = _const_spec((D_MODEL, D_MODEL))
    out_shape = (
        jax.ShapeDtypeStruct((n, D_MODEL), F32),
        jax.ShapeDtypeStruct((n, N_HEADS * AUG_DIM), BF16),
        jax.ShapeDtypeStruct((n, N_HEADS * AUG_DIM), BF16),
        jax.ShapeDtypeStruct((n, D_MODEL), F32),
        jax.ShapeDtypeStruct((n, D_MODEL), F32),
        jax.ShapeDtypeStruct((n // seq_rows, D_MODEL, seq_rows), BF16),
        jax.ShapeDtypeStruct((n, LANES), F32),
        jax.ShapeDtypeStruct((n, LANES), F32),
        jax.ShapeDtypeStruct((n, D_MODEL), BF16),
        jax.ShapeDtypeStruct((n, D_MODEL), BF16),
    )
    vt_spec = pl.BlockSpec((1, D_MODEL, tm), lambda i: (i // tiles_per_seq, 0, i % tiles_per_seq))
    return pl.pallas_call(
        functools.partial(_in_proj_kernel, inter=inter, tiles_per_seq=tiles_per_seq),
        grid=(n // tm,),
        in_specs=[row(D_MODEL), _const_spec((1, D_MODEL)), sq, sq, sq, sq,
                  _const_spec((D_MODEL, LANES)), sq, sq, _const_spec((1, LANES)),
                  _const_spec((LANES, D_MODEL)), _const_spec((LANES, D_MODEL))],
        out_specs=(row(D_MODEL), row(N_HEADS * AUG_DIM), row(N_HEADS * AUG_DIM), row(D_MODEL), row(D_MODEL),
                   vt_spec, row(LANES), row(LANES), row(D_MODEL), row(D_MODEL)),
        out_shape=out_shape,
        scratch_shapes=[pltpu.VMEM((SUBLANES, LANES), F32)],
        compiler_params=pltpu.CompilerParams(dimension_semantics=("arbitrary",), vmem_limit_bytes=VMEM_LIMIT),
        name="in_proj",
    )(x2, wts["g_mix_pre"], wts["w_lru"], wts["w_q"], wts["w_k"], wts["w_v"], wts["w_f"], wts["w_gl"],
      wts["w_ga"], wts["b_f"], pq, pk)


def _scan8(a, b):
    row = lax.broadcasted_iota(jnp.int32, a.shape, 0)
    for d in (1, 2, 4):
        keep = row >= d
        a_s = pltpu.roll(a, d, 0)
        b_s = pltpu.roll(b, d, 0)
        b = jnp.where(keep, a * b_s + b, b)
        a = jnp.where(keep, a * a_s, a)
    return a, b


def _lru_kernel(xl_ref, sgl_ref, hist_ref, h0_ref, cw_ref, cb_ref, wab_ref, ba_ref, bx_ref, lam_ref,
                og_ref, hl_ref,
                xp_ref, a_ref, b_ref, hcar_ref, *, inter, tiles_per_seq, steps):
    tt = xl_ref.shape[0]
    nh = (LRU_CONV_W - 1) * inter
    hp = xp_ref.shape[0] - tt
    i = pl.program_id(0)

    @pl.when(i % tiles_per_seq == 0)
    def _():
        xp_ref[hp - nh:hp, :] = hist_ref[0]
        hcar_ref[0:inter, :] = h0_ref[0]

    xp_ref[hp:hp + tt, :] = xl_ref[...]
    xc = cb_ref[...]
    for j in range(LRU_CONV_W):
        off = hp - (LRU_CONV_W - 1 - j) * inter
        xc = xc + cw_ref[j:j + 1, :] * xp_ref[off:off + tt, :]
    xp_ref[hp - nh:hp, :] = xp_ref[hp + tt - nh:hp + tt, :]

    xcb = xc.astype(BF16)
    sp = _softplus(-lam_ref[...])
    for blk in range(D_MODEL // MXU_DIM):
        cols = slice(blk * MXU_DIM, (blk + 1) * MXU_DIM)
        gates = _dot(xcb[:, cols], wab_ref[blk])
        r = _sigmoid(gates[:, :MXU_DIM] + ba_ref[:, cols])
        ig = _sigmoid(gates[:, MXU_DIM:] + bx_ref[:, cols])
        log_a = (-LRU_C * r) * sp[:, cols]
        a_ref[:, cols] = jnp.exp(log_a)
        th = jnp.tanh(log_a)
        b_ref[:, cols] = jnp.sqrt((-2.0 * th) / (1.0 - th)) * (ig * xc[:, cols])

    if inter == 1:
        def body(g, carry):
            r0 = pl.multiple_of(g * SUBLANES, SUBLANES)
            a8, b8 = _scan8(a_ref[pl.ds(r0, SUBLANES), :], b_ref[pl.ds(r0, SUBLANES), :])
            h8 = a8 * carry + b8
            b_ref[pl.ds(r0, SUBLANES), :] = h8
            return jnp.broadcast_to(h8[SUBLANES - 1:SUBLANES, :], (SUBLANES, D_MODEL))

        h_fin = lax.fori_loop(0, tt // SUBLANES, body,
                              jnp.broadcast_to(hcar_ref[0:1, :], (SUBLANES, D_MODEL)))
        hcar_ref[0:1, :] = h_fin[0:1, :]
    else:
        h = hcar_ref[0:inter, :]
        for t in range(steps):
            rows = slice(t * inter, (t + 1) * inter)
            h = a_ref[rows, :] * h + b_ref[rows, :]
            b_ref[rows, :] = h
        hcar_ref[0:inter, :] = h
    og_ref[...] = (sgl_ref[...].astype(F32) * b_ref[...]).astype(BF16)
    hl_ref[0] = hcar_ref[0:inter, :]


def _lru(xl, sgl, hist, h0, inter, seq_rows, wts):
    n = xl.shape[0]
    tt = min(ROW_TILE, seq_rows)
    tiles_per_seq = seq_rows // tt
    assert tiles_per_seq == 1 or inter == 1
    n_groups = n // seq_rows
    nh = (LRU_CONV_W - 1) * inter
    assert tt >= nh
    hp = _round_up(nh, SUBLANES)
    row = lambda w: pl.BlockSpec((tt, w), lambda i: (i, 0))
    grp = lambda r: pl.BlockSpec((1, r, D_MODEL), lambda i: (i // tiles_per_seq, 0, 0))
    vec = _const_spec((1, D_MODEL))
    return pl.pallas_call(
        functools.partial(_lru_kernel, inter=inter, tiles_per_seq=tiles_per_seq, steps=seq_rows // inter),
        grid=(n // tt,),
        in_specs=[row(D_MODEL), row(D_MODEL), grp(nh), grp(inter), _const_spec((LRU_CONV_W, D_MODEL)), vec,
                  _const_spec((D_MODEL // MXU_DIM, MXU_DIM, 2 * MXU_DIM)), vec, vec, vec],
        out_specs=(row(D_MODEL), grp(inter)),
        out_shape=(jax.ShapeDtypeStruct((n, D_MODEL), BF16),
                   jax.ShapeDtypeStruct((n_groups, inter, D_MODEL), F32)),
        scratch_shapes=[pltpu.VMEM((hp + tt, D_MODEL), F32), pltpu.VMEM((tt, D_MODEL), F32),
                        pltpu.VMEM((tt, D_MODEL), F32), pltpu.VMEM((max(inter, SUBLANES), D_MODEL), F32)],
        compiler_params=pltpu.CompilerParams(dimension_semantics=("arbitrary",), vmem_limit_bytes=VMEM_LIMIT),
        name="rg_lru",
    )(xl, sgl, hist, h0, wts["lru_cw"], wts["lru_cb"], wts["lru_wab"], wts["lru_ba"], wts["lru_bx"],
      wts["lru_lam"])


def _attn_prompt_kernel(q_ref, k_ref, vt_ref, o_ref, sa_ref, sb_ref, m_ref, l_ref, acc_ref):
    t = q_ref.shape[1]
    kc = min(ATT_KEY_CHUNK, t)
    i = pl.program_id(2)
    q = q_ref[0]

    def put_scores(j, dst):
        k0 = pl.multiple_of(j * t, t)
        dst[...] = _dot_nt(k_ref[0, pl.ds(k0, t), :], q)

    def update(j, src, diagonal):
        k0 = pl.multiple_of(j * t, t)
        if diagonal:
            kpos = lax.broadcasted_iota(jnp.int32, (t, t), 0)
            qpos = lax.broadcasted_iota(jnp.int32, (t, t), 1)
            src[...] = jnp.where(kpos <= qpos, src[...], -jnp.inf)
        m = m_ref[...]
        m_new = m
        for c in range(t // kc):
            m_new = jnp.maximum(m_new, jnp.max(src[c * kc:(c + 1) * kc, :], axis=0, keepdims=True))
        alpha = jnp.exp2(m - m_new)
        m_ref[...] = m_new
        l = alpha * l_ref[...]
        acc = alpha * acc_ref[...]
        for c in range(t // kc):
            p = jnp.exp2(src[c * kc:(c + 1) * kc, :] - m_new)
            l = l + jnp.sum(p, axis=0, keepdims=True)
            acc = acc + _dot(vt_ref[0, 0, :, pl.ds(k0 + c * kc, kc)], p.astype(BF16))
        l_ref[...] = l
        acc_ref[...] = acc

    m_ref[...] = jnp.full(m_ref.shape, -jnp.inf, F32)
    l_ref[...] = jnp.zeros(l_ref.shape, F32)
    acc_ref[...] = jnp.zeros(acc_ref.shape, F32)
    put_scores(0, sa_ref)

    def pair(jj, carry):
        j = 2 * jj
        put_scores(j + 1, sb_ref)
        update(j, sa_ref, False)
        put_scores(j + 2, sa_ref)
        update(j + 1, sb_ref, False)
        return carry

    lax.fori_loop(0, i // 2, pair, 0)

    @pl.when(i % 2 == 1)
    def _():
        put_scores(i, sb_ref)
        update(i - 1, sa_ref, False)
        update(i, sb_ref, True)

    @pl.when(i % 2 == 0)
    def _():
        update(i, sa_ref, True)

    o_ref[0] = (acc_ref[...] / l_ref[...]).T.astype(o_ref.dtype)


def _attn_prompt(qa, ka, vt):
    b, s, _ = qa.shape
    t = min(ATT_TILE, s)
    assert s % t == 0
    return pl.pallas_call(
        _attn_prompt_kernel,
        grid=(b, N_HEADS, s // t),
        in_specs=[pl.BlockSpec((1, t, AUG_DIM), lambda b_, h, i: (b_, i, h)),
                  pl.BlockSpec((1, s, AUG_DIM), lambda b_, h, i: (b_, 0, h)),
                  pl.BlockSpec((1, 1, HEAD_DIM, s), lambda b_, h, i: (b_, h, 0, 0))],
        out_specs=pl.BlockSpec((1, t, HEAD_DIM), lambda b_, h, i: (b_, i, h)),
        out_shape=jax.ShapeDtypeStruct((b, s, D_MODEL), BF16),
        scratch_shapes=[pltpu.VMEM((t, t), F32), pltpu.VMEM((t, t), F32), pltpu.VMEM((1, t), F32),
                        pltpu.VMEM((1, t), F32), pltpu.VMEM((HEAD_DIM, t), F32)],
        compiler_params=pltpu.CompilerParams(dimension_semantics=("arbitrary", "arbitrary", "arbitrary"),
                                             vmem_limit_bytes=VMEM_LIMIT),
        name="attn_prompt",
    )(qa, ka, vt)


KEYS_PER_GROUP = LANES // N_HEADS


def _attn_sample_kernel(pt_ref, *refs, n_pb, n_grp):
    k_refs = refs[0:n_pb]
    v_refs = refs[n_pb:2 * n_pb]
    f_refs = refs[2 * n_pb:3 * n_pb]
    (q_ref, kn_ref, vn_ref, cq_ref, ckn_ref, uin_ref, usum_ref, o_ref,
     qall_ref, mask_ref, m_ref, l_ref, acc_ref, tail_ref) = refs[3 * n_pb:]
    del pt_ref
    t_new = q_ref.shape[1]
    nq = N_HEADS * t_new
    page_w = n_grp * LANES
    p_idx = pl.program_id(1)

    def online(s, values):
        m_old = m_ref[...]
        m_new = jnp.maximum(m_old, jnp.max(s, axis=-1, keepdims=True))
        alpha = jnp.exp2(m_old - m_new)
        p = jnp.exp2(s - m_new)
        m_ref[...] = m_new
        l_ref[...] = alpha * l_ref[...] + jnp.sum(p, axis=-1, keepdims=True)
        pb = p.astype(BF16)
        pv = None
        for (lo, hi), xv in values:
            term = _dot(pb[:, lo:hi], xv)
            pv = term if pv is None else pv + term
        acc_ref[...] = alpha * acc_ref[...] + pv

    @pl.when(p_idx == 0)
    def _():
        row = lax.broadcasted_iota(jnp.int32, (nq, LANES), 0)
        lane = lax.broadcasted_iota(jnp.int32, (nq, LANES), 1)
        q = q_ref[0]
        qall_ref[...] = jnp.concatenate([q[:, h * HEAD_DIM:(h + 1) * HEAD_DIM] for h in range(N_HEADS)],
                                        axis=0).astype(BF16)
        same_head = (lane & (N_HEADS - 1)) == (row >> _log2(t_new))
        mask_ref[...] = jnp.where(same_head, cq_ref[0] * LOG2E, -jnp.inf)
        m_ref[...] = jnp.full(m_ref.shape, -jnp.inf, F32)
        l_ref[...] = jnp.zeros(l_ref.shape, F32)
        acc_ref[...] = jnp.zeros(acc_ref.shape, F32)
        tail_ref[...] = jnp.zeros(tail_ref.shape, F32)
        pad = LANES - nq
        kn = jnp.concatenate([kn_ref[0], jnp.zeros((pad, HEAD_DIM), F32)], axis=0).astype(BF16)
        vn = jnp.concatenate([vn_ref[0], jnp.zeros((pad, HEAD_DIM), F32)], axis=0).astype(BF16)
        s = _dot_nt(qall_ref[...], kn) + (mask_ref[...] - ckn_ref[0] * LOG2E)
        s = jnp.where((lane >> _log2(N_HEADS)) <= (row & (t_new - 1)), s, -jnp.inf)
        online(s, [((0, LANES), vn)])

    qall = qall_ref[...]
    mask = mask_ref[...]
    tail = tail_ref[...]
    zeros8 = jnp.zeros((SUBLANES, LANES), F32)
    grp_row = lax.broadcasted_iota(jnp.int32, (n_grp, LANES), 0)
    s_parts = []
    for u in range(n_pb):
        lp = f_refs[u][0]
        hi, mid, lo = _split3(lp)
        l3 = jnp.concatenate([hi, mid, lo, zeros8], axis=0).astype(BF16)
        pin = _dot(l3, uin_ref[...])
        psum = _dot(l3, usum_ref[...])
        intra = pin[0:n_grp] + pin[n_grp:2 * n_grp] + pin[2 * n_grp:3 * n_grp]
        tot = psum[0:n_grp] + psum[n_grp:2 * n_grp] + psum[2 * n_grp:3 * n_grp]
        r = tot
        d = 1
        while d < n_grp:
            r = r + jnp.where(grp_row + d < n_grp, pltpu.roll(r, n_grp - d, 0), 0.0)
            d *= 2
        suf = (intra + (r - tot) + tail) * LOG2E
        tail = tail + r[0:1, :]
        sk = _dot_nt(qall, k_refs[u][0].astype(BF16))
        s_parts += [sk[:, g * LANES:(g + 1) * LANES] + (mask + suf[g:g + 1, :]) for g in range(n_grp)]
    tail_ref[...] = tail
    online(jnp.concatenate(s_parts, axis=1),
           [((u * page_w, (u + 1) * page_w), v_refs[u][0].astype(BF16)) for u in range(n_pb)])

    @pl.when(p_idx == pl.num_programs(1) - 1)
    def _():
        o = acc_ref[...] / l_ref[...]
        for h in range(N_HEADS):
            o_ref[0, :, h * HEAD_DIM:(h + 1) * HEAD_DIM] = o[h * t_new:(h + 1) * t_new, :]


def _suffix_matrices():
    idx = np.arange(LANES)
    key, head = idx // N_HEADS, idx % N_HEADS
    same = head[:, None] == head[None, :]
    later = same & (key[:, None] > key[None, :])
    return jnp.asarray(later, BF16), jnp.asarray(same, BF16)


def _attn_sample(page_ids, cache_k, cache_v, cache_logf, q, kn, vn, cq, ckn):
    db, n_pages = page_ids.shape
    page_rows = cache_k.shape[1]
    n_grp = page_rows // LANES
    t_new = q.shape[1]
    nq = N_HEADS * t_new
    assert t_new == SUBLANES and n_grp == SUBLANES and cache_logf.shape[1:] == (n_grp, LANES)
    n_pb = PAGES_PER_STEP if n_pages % PAGES_PER_STEP == 0 else 1
    later, same = _suffix_matrices()

    def page_map(u):
        def index_map(b, p, pt):
            return (pt[b * n_pages + (n_pages - 1 - (p * n_pb + u))], 0, 0)
        return index_map

    per_b = lambda r, w: pl.BlockSpec((1, r, w), lambda b, p, pt: (b, 0, 0))
    const = pl.BlockSpec((LANES, LANES), lambda b, p, pt: (0, 0))
    in_specs = ([pl.BlockSpec((1, page_rows, HEAD_DIM), page_map(u)) for u in range(n_pb)]
                + [pl.BlockSpec((1, page_rows, HEAD_DIM), page_map(u)) for u in range(n_pb)]
                + [pl.BlockSpec((1, n_grp, LANES), page_map(u)) for u in range(n_pb)]
                + [per_b(t_new, D_MODEL), per_b(nq, HEAD_DIM), per_b(nq, HEAD_DIM), per_b(nq, 1), per_b(1, LANES),
                   const, const])
    grid_spec = pltpu.PrefetchScalarGridSpec(
        num_scalar_prefetch=1,
        grid=(db, n_pages // n_pb),
        in_specs=in_specs,
        out_specs=pl.BlockSpec((1, t_new, D_MODEL), lambda b, p, pt: (b, 0, 0)),
        scratch_shapes=[pltpu.VMEM((nq, HEAD_DIM), BF16), pltpu.VMEM((nq, LANES), F32), pltpu.VMEM((nq, 1), F32),
                        pltpu.VMEM((nq, 1), F32), pltpu.VMEM((nq, HEAD_DIM), F32), pltpu.VMEM((1, LANES), F32)],
    )
    return pl.pallas_call(
        functools.partial(_attn_sample_kernel, n_pb=n_pb, n_grp=n_grp),
        grid_spec=grid_spec,
        out_shape=jax.ShapeDtypeStruct((db, t_new, D_MODEL), F32),
        compiler_params=pltpu.CompilerParams(dimension_semantics=("arbitrary", "arbitrary"),
                                             vmem_limit_bytes=VMEM_LIMIT),
        name="attn_sample",
    )(page_ids.reshape(-1), *([cache_k] * n_pb), *([cache_v] * n_pb), *([cache_logf] * n_pb),
      q, kn, vn, cq, ckn, later, same)


def _gelu_tanh(x):
    return 0.5 * x * (1.0 + jnp.tanh(np.float32(np.sqrt(2.0 / np.pi)) * (x + 0.044715 * (x * x * x))))


def _post_kernel(x_ref, og_ref, oa_ref, sga_ref, p_ref, hist_ref,
                 wo_ref, gpost_ref, gpre_ref, wup_ref, fcw_ref, fcb_ref, wdn_ref, gfpost_ref,
                 wple_ref, wpg_ref, gple_ref,
                 y_ref, hout_ref,
                 hh_ref, xg_ref, xv_ref, *, inter, tiles_per_seq):
    tm = x_ref.shape[0]
    nh = (FFN_CONV_W - 1) * inter
    hp = hh_ref.shape[0]
    i = pl.program_id(0)

    @pl.when(i % tiles_per_seq == 0)
    def _():
        if hp > nh:
            hh_ref[...] = jnp.zeros_like(hh_ref)
        hh_ref[hp - nh:hp, :] = hist_ref[0]

    merged = og_ref[...].astype(F32) + sga_ref[...].astype(F32) * oa_ref[...].astype(F32)
    x1 = x_ref[...] + _rms(_dot(merged.astype(BF16), wo_ref[...]), gpost_ref[...])
    hn = _rms(x1, gpre_ref[...]).astype(BF16)

    y2 = jnp.zeros((tm, D_MODEL), F32)
    for c in range(D_FF // FFN_CHUNK):
        halves = []
        for xp_ref, base in ((xg_ref, 0), (xv_ref, D_FF)):
            cols = slice(base + c * FFN_CHUNK, base + (c + 1) * FFN_CHUNK)
            xp_ref[0:hp, :] = hh_ref[:, cols]
            xp_ref[hp:hp + tm, :] = _dot(hn, wup_ref[:, cols])
            hc = fcb_ref[:, cols]
            for j in range(FFN_CONV_W):
                off = hp - (FFN_CONV_W - 1 - j) * inter
                hc = hc + fcw_ref[j:j + 1, cols] * xp_ref[off:off + tm, :]
            hh_ref[:, cols] = xp_ref[tm:tm + hp, :]
            halves.append(hc)
        act = (_gelu_tanh(halves[0]) * halves[1]).astype(BF16)
        y2 = y2 + _dot(act, wdn_ref[c * FFN_CHUNK:(c + 1) * FFN_CHUNK, :])
    hout_ref[0] = hh_ref[hp - nh:hp, :]

    x2 = x1 + _rms(y2, gfpost_ref[...])
    e = _dot(p_ref[...].astype(BF16), wple_ref[...]) * _sigmoid(_dot(x2.astype(BF16), wpg_ref[...]))
    y_ref[...] = x2 + _rms(e, gple_ref[...])


def _post(x2, og, oa, sga, p2, hist, inter, seq_rows, wts):
    n = x2.shape[0]
    tm = min(ROW_TILE, seq_rows)
    tiles_per_seq = seq_rows // tm
    assert tiles_per_seq == 1 or inter == 1
    n_groups = n // seq_rows
    nh = (FFN_CONV_W - 1) * inter
    hp = _round_up(nh, SUBLANES)
    assert tm >= hp
    d_ple = p2.shape[1]
    row = lambda w: pl.BlockSpec((tm, w), lambda i: (i, 0))
    grp = pl.BlockSpec((1, nh, 2 * D_FF), lambda i: (i // tiles_per_seq, 0, 0))
    vec = _const_spec((1, D_MODEL))
    return pl.pallas_call(
        functools.partial(_post_kernel, inter=inter, tiles_per_seq=tiles_per_seq),
        grid=(n // tm,),
        in_specs=[row(D_MODEL), row(D_MODEL), row(D_MODEL), row(D_MODEL), row(d_ple), grp,
                  _const_spec((D_MODEL, D_MODEL)), vec, vec, _const_spec((D_MODEL, 2 * D_FF)),
                  _const_spec((FFN_CONV_W, 2 * D_FF)), _const_spec((1, 2 * D_FF)), _const_spec((D_FF, D_MODEL)), vec,
                  _const_spec((d_ple, D_MODEL)), _const_spec((D_MODEL, D_MODEL)), vec],
        out_specs=(row(D_MODEL), grp),
        out_shape=(jax.ShapeDtypeStruct((n, D_MODEL), F32),
                   jax.ShapeDtypeStruct((n_groups, nh, 2 * D_FF), F32)),
        scratch_shapes=[pltpu.VMEM((hp, 2 * D_FF), F32), pltpu.VMEM((hp + tm, FFN_CHUNK), F32),
                        pltpu.VMEM((hp + tm, FFN_CHUNK), F32)],
        compiler_params=pltpu.CompilerParams(dimension_semantics=("arbitrary",), vmem_limit_bytes=VMEM_LIMIT),
        name="post_ffn",
    )(x2, og, oa, sga, p2, hist, wts["w_out"], wts["g_mix_post"], wts["g_ffn_pre"], wts["w_up"], wts["ffn_cw"],
      wts["ffn_cb"], wts["w_down"], wts["g_ffn_post"], wts["w_ple"], wts["w_ple_gate"], wts["g_ple_post"])


def _prep_weights(l, norm_mix_pre, w_in, b_forget, lru_conv_w, lru_conv_b, lru_wa, lru_ba, lru_wx, lru_bx,
                  lru_lambda, w_out, norm_mix_post, norm_ffn_pre, w_up, ffn_conv_w, ffn_conv_b, w_down,
                  norm_ffn_post, w_ple, w_ple_gate, norm_ple_post):
    d = D_MODEL
    wi = w_in[l]
    offs = np.cumsum([0, d, d, d, d, N_HEADS, d, d])
    piece = lambda k: wi[:, int(offs[k]):int(offs[k + 1])].astype(BF16)
    vec = lambda a: a[l].reshape(1, -1).astype(F32)
    per_tile = MXU_DIM // LRU_BLOCK
    n_tiles = LRU_HEADS // per_tile

    def block_diag(w):
        w4 = w.reshape(n_tiles, per_tile, LRU_BLOCK, LRU_BLOCK)
        eye = jnp.eye(per_tile, dtype=w.dtype)
        return jnp.einsum("thij,hg->thigj", w4, eye).reshape(n_tiles, MXU_DIM, MXU_DIM)

    return {
        "g_mix_pre": vec(norm_mix_pre),
        "w_lru": piece(0), "w_q": piece(1), "w_k": piece(2), "w_v": piece(3),
        "w_f": jnp.pad(piece(4), ((0, 0), (0, LANES - N_HEADS))),
        "w_gl": piece(5), "w_ga": piece(6),
        "b_f": jnp.pad(vec(b_forget), ((0, 0), (0, LANES - N_HEADS))),
        "lru_cw": lru_conv_w[l].astype(F32), "lru_cb": vec(lru_conv_b),
        "lru_wab": jnp.concatenate([block_diag(lru_wa[l]), block_diag(lru_wx[l])], axis=-1).astype(BF16),
        "lru_ba": vec(lru_ba), "lru_bx": vec(lru_bx), "lru_lam": vec(lru_lambda),
        "w_out": w_out[l].astype(BF16), "g_mix_post": vec(norm_mix_post), "g_ffn_pre": vec(norm_ffn_pre),
        "w_up": w_up[l].astype(BF16), "ffn_cw": ffn_conv_w[l].astype(F32), "ffn_cb": vec(ffn_conv_b),
        "w_down": w_down[l].astype(BF16), "g_ffn_post": vec(norm_ffn_post),
        "w_ple": w_ple[l].astype(BF16), "w_ple_gate": w_ple_gate[l].astype(BF16), "g_ple_post": vec(norm_ple_post),
    }


def _tail_rows(buf, new, width):
    return jnp.concatenate([buf.astype(new.dtype), new], axis=1)[:, -width:]


def _prompt_layer(x, p_l, wts):
    b, s, d = x.shape
    n = b * s
    xl, qa, ka, kf, vf, vt, lf, _, sgl, sga = _in_proj(x.reshape(n, d), 1, s, wts)
    og, h_last = _lru(xl, sgl, jnp.zeros((b, LRU_CONV_W - 1, d), F32), jnp.zeros((b, 1, d), F32), 1, s, wts)
    oa = _attn_prompt(qa.reshape(b, s, -1), ka.reshape(b, s, -1), vt.reshape(b, N_HEADS, HEAD_DIM, s))
    y, ffn_hist = _post(x.reshape(n, d), og, oa.reshape(n, d), sga, p_l.reshape(n, -1),
                        jnp.zeros((b, FFN_CONV_W - 1, 2 * D_FF), F32), 1, s, wts)
    state = (kf.reshape(b, s, N_HEADS, HEAD_DIM), vf.reshape(b, s, N_HEADS, HEAD_DIM),
             lf[:, :N_HEADS].reshape(b, s, N_HEADS), h_last.reshape(b, d),
             _tail_rows(jnp.zeros((b, LRU_CONV_W - 1, d), F32), xl.reshape(b, s, d), LRU_CONV_W - 1),
             ffn_hist)
    return y.reshape(b, s, d), state


def _sample_layer(x, p_l, cache_k, cache_v, cache_logf, page_ids, lru_buf, lru_h0, ffn_buf, wts):
    db, t, d = x.shape
    n = db * t
    tmaj = lambda a: jnp.swapaxes(a, 0, 1).reshape(a.shape[0] * a.shape[1], -1)
    bmaj = lambda a: jnp.swapaxes(a.reshape(-1, db, a.shape[-1]), 0, 1)

    xt = tmaj(x)
    xl, qa, _, kf, vf, _, lf, c, sgl, sga = _in_proj(xt, db, n, wts)
    og, h_last = _lru(xl, sgl, tmaj(lru_buf)[None], lru_h0[None], db, n, wts)

    k_b, v_b, c_b = bmaj(kf), bmaj(vf), bmaj(c)[..., :N_HEADS]
    q_b = bmaj(qa).astype(F32).reshape(db, t, N_HEADS, AUG_DIM)[..., :HEAD_DIM].reshape(db, t, d)
    cq = jnp.swapaxes(c_b, 1, 2).reshape(db, N_HEADS * t, 1)
    ckn = jnp.pad(c_b.reshape(db, 1, t * N_HEADS), ((0, 0), (0, 0), (0, LANES - t * N_HEADS)))
    oa_b = _attn_sample(page_ids, cache_k, cache_v, cache_logf, q_b,
                        k_b.reshape(db, t * N_HEADS, HEAD_DIM), v_b.reshape(db, t * N_HEADS, HEAD_DIM), cq, ckn)

    y, ffn_hist = _post(xt, og, tmaj(oa_b), sga, tmaj(p_l), tmaj(ffn_buf)[None], db, n, wts)
    state = (k_b.reshape(db, t, N_HEADS, HEAD_DIM), v_b.reshape(db, t, N_HEADS, HEAD_DIM),
             bmaj(lf)[..., :N_HEADS], h_last.reshape(db, d),
             _tail_rows(lru_buf, bmaj(xl), LRU_CONV_W - 1),
             bmaj(ffn_hist.reshape(-1, 2 * D_FF)))
    return bmaj(y), state


def kernel(x_prompt, x_sample, cache_k, cache_v, cache_logf, state_lru_h, state_lru_conv, state_ffn_conv, page_table, p_prompt, p_sample, norm_mix_pre, w_in, b_forget, lru_conv_w, lru_conv_b, lru_wa, lru_ba, lru_wx, lru_bx, lru_lambda, w_out, norm_mix_post, norm_ffn_pre, w_up, ffn_conv_w, ffn_conv_b, w_down, norm_ffn_post, w_ple, w_ple_gate, norm_ple_post):
    depth, n_pool, page = cache_k.shape[:3]
    ck = cache_k.reshape(depth * n_pool, page * N_HEADS, HEAD_DIM)
    cv = cache_v.reshape(depth * n_pool, page * N_HEADS, HEAD_DIM)
    cf = cache_logf.reshape(depth * n_pool, page * N_HEADS // LANES, LANES)
    y_p, y_s = x_prompt, x_sample
    outs_p, outs_s = [], []
    for l in range(depth):
        wts = _prep_weights(l, norm_mix_pre, w_in, b_forget, lru_conv_w, lru_conv_b, lru_wa, lru_ba, lru_wx, lru_bx,
                            lru_lambda, w_out, norm_mix_post, norm_ffn_pre, w_up, ffn_conv_w, ffn_conv_b, w_down,
                            norm_ffn_post, w_ple, w_ple_gate, norm_ple_post)
        y_p, st_p = _prompt_layer(y_p, p_prompt[l], wts)
        y_s, st_s = _sample_layer(y_s, p_sample[l], ck, cv, cf, page_table + l * n_pool,
                                  state_lru_conv[l], state_lru_h[l], state_ffn_conv[l], wts)
        outs_p.append(st_p)
        outs_s.append(st_s)
    stack = lambda outs, k: jnp.stack([o[k] for o in outs])
    res = [y_p, y_s]
    for k in range(6):
        res.append(stack(outs_p, k))
        res.append(stack(outs_s, k))
    return tuple(res)
```

```python
import functools

import numpy as np
import jax
import jax.numpy as jnp
from jax import lax
from jax.experimental import pallas as pl
from jax.experimental.pallas import tpu as pltpu

D_MODEL = 1024
N_HEADS = 8
HEAD_DIM = 128
ATT_SCALE = HEAD_DIM ** -0.5
LRU_HEADS = 16
LRU_BLOCK = 64
LRU_CONV_W = 4
LRU_C = 8.0
D_FF = 3 * D_MODEL
FFN_CONV_W = 3
RMS_EPS = 1e-6

LANES = 128
SUBLANES = 8
MXU_DIM = 256
AUG_DIM = 2 * HEAD_DIM
VMEM_LIMIT = 56 * 1024 * 1024

ROW_TILE = 256
ATT_TILE = 1024
ATT_KEY_CHUNK = MXU_DIM
FFN_CHUNK = 1024
PAGES_PER_STEP = 8
LOG2E = float(np.log2(np.e))

F32 = jnp.float32
BF16 = jnp.bfloat16


def _round_up(n, m):
    return (n + m - 1) // m * m


def _log2(n):
    assert n > 0 and n & (n - 1) == 0
    return n.bit_length() - 1


def _rms(x, g):
    ms = jnp.mean(x * x, axis=-1, keepdims=True)
    return (x * lax.rsqrt(ms + RMS_EPS)) * g


def _sigmoid(x):
    return 1.0 / (1.0 + jnp.exp(-x))


def _softplus(x):
    return jnp.maximum(x, 0.0) + jnp.log1p(jnp.exp(-jnp.abs(x)))


def _dot(a, b):
    return jnp.dot(a, b, preferred_element_type=F32)


def _dot_nt(a, b):
    return lax.dot_general(a, b, (((1,), (1,)), ((), ())), preferred_element_type=F32)


def _split3(c):
    hi = c.astype(BF16).astype(F32)
    r1 = c - hi
    mid = r1.astype(BF16).astype(F32)
    lo = (r1 - mid).astype(BF16).astype(F32)
    return hi, mid, lo


def _const_spec(shape):
    nd = len(shape)
    return pl.BlockSpec(shape, lambda *_: (0,) * nd, pipeline_mode=pl.Buffered(1))


def _in_proj_kernel(x_ref, g_ref, wl_ref, wq_ref, wk_ref, wv_ref, wf_ref, wgl_ref, wga_ref, bf_ref,
                    pq_ref, pk_ref, hist_ref, h0_ref, cw_ref, cb_ref, wab_ref, ba_ref, bx_ref, lam_ref,
                    qa_ref, ka_ref, kf_ref, vf_ref, vt_ref, lf_ref, c_ref, sga_ref, og_ref, hl_ref, hout_ref,
                    carry_ref, xp_ref, hcar_ref, *, inter, tiles_per_seq, steps):
    tm = x_ref.shape[0]
    i = pl.program_id(0)
    ub = _rms(x_ref[...], g_ref[...]).astype(BF16)

    lru = _lru_tile(ub, wl_ref, wgl_ref, i, hist_ref, h0_ref, cw_ref, cb_ref, wab_ref, ba_ref, bx_ref, lam_ref,
                    og_ref, hl_ref, hout_ref, xp_ref, hcar_ref, inter=inter, tiles_per_seq=tiles_per_seq, steps=steps)
    next(lru)
    kf = _dot(ub, wk_ref[...])
    kf_ref[...] = kf
    next(lru)
    vf = _dot(ub, wv_ref[...])
    vf_ref[...] = vf
    vt_ref[0] = vf.T.astype(BF16)
    next(lru)
    sga_ref[...] = _sigmoid(_dot(ub, wga_ref[...])).astype(BF16)
    next(lru)
    qs = (_dot(ub, wq_ref[...]) * (ATT_SCALE * LOG2E)).astype(BF16)
    kb = kf.astype(BF16)

    lf = -_softplus(-(_dot(ub, wf_ref[...]) + bf_ref[...]))
    lf_ref[...] = lf
    row = lax.broadcasted_iota(jnp.int32, (tm, LANES), 0)
    c = lf
    d = inter
    while d < tm:
        c = c + jnp.where(row >= d, pltpu.roll(c, d, 0), 0.0)
        d *= 2
    if tiles_per_seq > 1:
        @pl.when(i % tiles_per_seq == 0)
        def _():
            carry_ref[...] = jnp.zeros_like(carry_ref)
        c = c + carry_ref[0:1, :]
        carry_ref[0:1, :] = c[tm - 1:tm, :]
    c_ref[...] = c

    hi, mid, lo = _split3(c * LOG2E)
    lane = lax.broadcasted_iota(jnp.int32, (tm, LANES), 1)
    cp = jnp.where(lane < N_HEADS, hi,
                   jnp.where(lane < 2 * N_HEADS, pltpu.roll(mid, N_HEADS, 1),
                             jnp.where(lane < 3 * N_HEADS, pltpu.roll(lo, 2 * N_HEADS, 1),
                                       jnp.where(lane == 3 * N_HEADS, 1.0, 0.0)))).astype(BF16)
    qx = _dot(cp, pq_ref[...]).astype(BF16)
    kx = _dot(cp, pk_ref[...]).astype(BF16)
    for h in range(N_HEADS):
        src = slice(h * HEAD_DIM, (h + 1) * HEAD_DIM)
        qa_ref[:, h * AUG_DIM:h * AUG_DIM + HEAD_DIM] = qs[:, src]
        qa_ref[:, h * AUG_DIM + HEAD_DIM:(h + 1) * AUG_DIM] = qx[:, src]
        ka_ref[:, h * AUG_DIM:h * AUG_DIM + HEAD_DIM] = kb[:, src]
        ka_ref[:, h * AUG_DIM + HEAD_DIM:(h + 1) * AUG_DIM] = kx[:, src]
    for _ in lru:
        pass


def _bias_scatter_matrices():
    pq = np.zeros((LANES, D_MODEL), np.float32)
    pk = np.zeros((LANES, D_MODEL), np.float32)
    for h in range(N_HEADS):
        base = h * HEAD_DIM
        for part in range(3):
            pq[part * N_HEADS + h, base + part] = 1.0
            pq[3 * N_HEADS, base + 3 + part] = 1.0
            pk[3 * N_HEADS, base + part] = 1.0
            pk[part * N_HEADS + h, base + 3 + part] = -1.0
    return jnp.asarray(pq, BF16), jnp.asarray(pk, BF16)


def _in_proj(x2, hist, h0, inter, seq_rows, wts):
    n = x2.shape[0]
    tm = min(ROW_TILE, seq_rows)
    assert n % tm == 0 and seq_rows % tm == 0
    tiles_per_seq = seq_rows // tm
    assert tiles_per_seq == 1 or inter == 1
    n_groups = n // seq_rows
    nh = (LRU_CONV_W - 1) * inter
    assert tm >= nh
    hp = _round_up(nh, SUBLANES)
    pq, pk = _bias_scatter_matrices()
    row = lambda w: pl.BlockSpec((tm, w), lambda i: (i, 0))
    grp = lambda r: pl.BlockSpec((1, r, D_MODEL), lambda i: (i // tiles_per_seq, 0, 0))
    sq = _const_spec((D_MODEL, D_MODEL))
    vec = _const_spec((1, D_MODEL))
    out_shape = (
        jax.ShapeDtypeStruct((n, N_HEADS * AUG_DIM), BF16),
        jax.ShapeDtypeStruct((n, N_HEADS * AUG_DIM), BF16),
        jax.ShapeDtypeStruct((n, D_MODEL), F32),
        jax.ShapeDtypeStruct((n, D_MODEL), F32),
        jax.ShapeDtypeStruct((n_groups, D_MODEL, seq_rows), BF16),
        jax.ShapeDtypeStruct((n, LANES), F32),
        jax.ShapeDtypeStruct((n, LANES), F32),
        jax.ShapeDtypeStruct((n, D_MODEL), BF16),
        jax.ShapeDtypeStruct((n, D_MODEL), BF16),
        jax.ShapeDtypeStruct((n_groups, inter, D_MODEL), F32),
        jax.ShapeDtypeStruct((n_groups, nh, D_MODEL), F32),
    )
    vt_spec = pl.BlockSpec((1, D_MODEL, tm), lambda i: (i // tiles_per_seq, 0, i % tiles_per_seq))
    return pl.pallas_call(
        functools.partial(_in_proj_kernel, inter=inter, tiles_per_seq=tiles_per_seq, steps=seq_rows // inter),
        grid=(n // tm,),
        in_specs=[row(D_MODEL), vec, sq, sq, sq, sq,
                  _const_spec((D_MODEL, LANES)), sq, sq, _const_spec((1, LANES)),
                  _const_spec((LANES, D_MODEL)), _const_spec((LANES, D_MODEL)),
                  grp(nh), grp(inter), _const_spec((LRU_CONV_W, D_MODEL)), vec,
                  _const_spec((D_MODEL // MXU_DIM, MXU_DIM, 2 * MXU_DIM)), vec, vec, vec],
        out_specs=(row(N_HEADS * AUG_DIM), row(N_HEADS * AUG_DIM), row(D_MODEL), row(D_MODEL),
                   vt_spec, row(LANES), row(LANES), row(D_MODEL), row(D_MODEL), grp(inter), grp(nh)),
        out_shape=out_shape,
        scratch_shapes=[pltpu.VMEM((SUBLANES, LANES), F32),
                        pltpu.VMEM((hp + tm, D_MODEL), F32), pltpu.VMEM((max(inter, SUBLANES), D_MODEL), F32)],
        compiler_params=pltpu.CompilerParams(dimension_semantics=("arbitrary",), vmem_limit_bytes=VMEM_LIMIT),
        name="in_proj",
    )(x2, wts["g_mix_pre"], wts["w_lru"], wts["w_q"], wts["w_k"], wts["w_v"], wts["w_f"], wts["w_gl"],
      wts["w_ga"], wts["b_f"], pq, pk, hist, h0, wts["lru_cw"], wts["lru_cb"], wts["lru_wab"], wts["lru_ba"],
      wts["lru_bx"], wts["lru_lam"])


def _scan8(a, b):
    row = lax.broadcasted_iota(jnp.int32, a.shape, 0)
    for d in (1, 2, 4):
        keep = row >= d
        a_s = pltpu.roll(a, d, 0)
        b_s = pltpu.roll(b, d, 0)
        b = jnp.where(keep, a * b_s + b, b)
        a = jnp.where(keep, a * a_s, a)
    return a, b


def _lru_tile(ub, wl_ref, wgl_ref, i, hist_ref, h0_ref, cw_ref, cb_ref, wab_ref, ba_ref, bx_ref, lam_ref,
              og_ref, hl_ref, hout_ref, xp_ref, hcar_ref, *, inter, tiles_per_seq, steps):
    tt = ub.shape[0]
    nh = (LRU_CONV_W - 1) * inter
    hp = xp_ref.shape[0] - tt

    @pl.when(i % tiles_per_seq == 0)
    def _():
        xp_ref[hp - nh:hp, :] = hist_ref[0]
        hcar_ref[0:inter, :] = h0_ref[0]

    sp = _softplus(-lam_ref[...])
    for blk in range(D_MODEL // MXU_DIM):
        cols = slice(blk * MXU_DIM, (blk + 1) * MXU_DIM)
        xp_ref[hp:hp + tt, cols] = _dot(ub, wl_ref[:, cols])
        xc = cb_ref[:, cols]
        for j in range(LRU_CONV_W):
            off = hp - (LRU_CONV_W - 1 - j) * inter
            xc = xc + cw_ref[j:j + 1, cols] * xp_ref[off:off + tt, cols]
        xp_ref[hp - nh:hp, cols] = xp_ref[hp + tt - nh:hp + tt, cols]
        hout_ref[0, :, cols] = xp_ref[hp - nh:hp, cols]

        gates = _dot(xc.astype(BF16), wab_ref[blk])
        r = _sigmoid(gates[:, :MXU_DIM] + ba_ref[:, cols])
        ig = _sigmoid(gates[:, MXU_DIM:] + bx_ref[:, cols])
        log_a = (-LRU_C * r) * sp[:, cols]
        a = jnp.exp(log_a)
        th = jnp.tanh(log_a)
        b = jnp.sqrt((-2.0 * th) / (1.0 - th)) * (ig * xc)
        hs = []
        if inter == 1:
            carry = jnp.broadcast_to(hcar_ref[0:1, cols], (SUBLANES, MXU_DIM))
            for g in range(tt // SUBLANES):
                rows = slice(g * SUBLANES, (g + 1) * SUBLANES)
                a8, b8 = _scan8(a[rows, :], b[rows, :])
                h8 = a8 * carry + b8
                hs.append(h8)
                carry = jnp.broadcast_to(h8[SUBLANES - 1:SUBLANES, :], (SUBLANES, MXU_DIM))
            hcar_ref[0:1, cols] = carry[0:1, :]
        else:
            h = hcar_ref[0:inter, cols]
            for t in range(steps):
                rows = slice(t * inter, (t + 1) * inter)
                h = a[rows, :] * h + b[rows, :]
                hs.append(h)
            hcar_ref[0:inter, cols] = h
        sgl = _sigmoid(_dot(ub, wgl_ref[:, cols]))
        og_ref[:, cols] = (sgl * jnp.concatenate(hs, axis=0)).astype(BF16)
        yield
    hl_ref[0] = hcar_ref[0:inter, :]


def _attn_prompt_kernel(q_ref, k_ref, vt_ref, o_ref, sa_ref, sb_ref, m_ref, l_ref, acc_ref):
    tq = q_ref.shape[1]
    tk = sa_ref.shape[0]
    kc = min(ATT_KEY_CHUNK, tk)
    i = pl.program_id(2)

    def put_scores(j, dst, lo=0):
        k0 = pl.multiple_of(j * tk, tk)
        dst[:, lo:] = _dot_nt(k_ref[0, pl.ds(k0, tk), :], q_ref[0, lo:, :])

    def update(j, src, lo=0, diagonal=False):
        k0 = pl.multiple_of(j * tk, tk)
        if diagonal:
            kpos = lax.broadcasted_iota(jnp.int32, (tk, tk), 0)
            qpos = lax.broadcasted_iota(jnp.int32, (tk, tk), 1)
            src[:, lo:lo + tk] = jnp.where(kpos <= qpos, src[:, lo:lo + tk], -jnp.inf)
        m = m_ref[:, lo:]
        m_new = m
        for c in range(tk // kc):
            m_new = jnp.maximum(m_new, jnp.max(src[c * kc:(c + 1) * kc, lo:], axis=0, keepdims=True))
        alpha = jnp.exp2(m - m_new)
        m_ref[:, lo:] = m_new
        l = alpha * l_ref[:, lo:]
        acc = alpha * acc_ref[:, lo:]
        for c in range(tk // kc):
            p = jnp.exp2(src[c * kc:(c + 1) * kc, lo:] - m_new)
            l = l + jnp.sum(p, axis=0, keepdims=True)
            acc = acc + _dot(vt_ref[0, 0, :, pl.ds(k0 + c * kc, kc)], p.astype(BF16))
        l_ref[:, lo:] = l
        acc_ref[:, lo:] = acc

    m_ref[...] = jnp.full(m_ref.shape, -jnp.inf, F32)
    l_ref[...] = jnp.zeros(l_ref.shape, F32)
    acc_ref[...] = jnp.zeros(acc_ref.shape, F32)
    put_scores(0, sa_ref)

    def pair(jj, carry):
        j = 2 * jj
        put_scores(j + 1, sb_ref)
        update(j, sa_ref)
        put_scores(j + 2, sa_ref)
        update(j + 1, sb_ref)
        return carry

    lax.fori_loop(0, i, pair, 0)
    put_scores(2 * i + 1, sb_ref, lo=tk)
    update(2 * i, sa_ref, diagonal=True)
    update(2 * i + 1, sb_ref, lo=tk, diagonal=True)
    o_ref[0] = (acc_ref[...] / l_ref[...]).T.astype(o_ref.dtype)


def _attn_prompt(qa, ka, vt):
    b, s, _ = qa.shape
    tq = min(ATT_TILE, s)
    tk = tq // 2
    assert s % tq == 0 and tk % LANES == 0
    return pl.pallas_call(
        _attn_prompt_kernel,
        grid=(b, N_HEADS, s // tq),
        in_specs=[pl.BlockSpec((1, tq, AUG_DIM), lambda b_, h, i: (b_, i, h)),
                  pl.BlockSpec((1, s, AUG_DIM), lambda b_, h, i: (b_, 0, h)),
                  pl.BlockSpec((1, 1, HEAD_DIM, s), lambda b_, h, i: (b_, h, 0, 0))],
        out_specs=pl.BlockSpec((1, tq, HEAD_DIM), lambda b_, h, i: (b_, i, h)),
        out_shape=jax.ShapeDtypeStruct((b, s, D_MODEL), BF16),
        scratch_shapes=[pltpu.VMEM((tk, tq), F32), pltpu.VMEM((tk, tq), F32), pltpu.VMEM((1, tq), F32),
                        pltpu.VMEM((1, tq), F32), pltpu.VMEM((HEAD_DIM, tq), F32)],
        compiler_params=pltpu.CompilerParams(dimension_semantics=("arbitrary", "arbitrary", "arbitrary"),
                                             vmem_limit_bytes=VMEM_LIMIT),
        name="attn_prompt",
    )(qa, ka, vt)


KEYS_PER_GROUP = LANES // N_HEADS


def _attn_sample_kernel(pt_ref, *refs, n_pb, n_grp):
    k_refs = refs[0:n_pb]
    v_refs = refs[n_pb:2 * n_pb]
    f_refs = refs[2 * n_pb:3 * n_pb]
    (q_ref, kn_ref, vn_ref, cq_ref, ckn_ref, uin_ref, usum_ref, o_ref,
     qall_ref, mask_ref, m_ref, l_ref, acc_ref, tail_ref) = refs[3 * n_pb:]
    del pt_ref
    t_new = q_ref.shape[1]
    nq = N_HEADS * t_new
    page_w = n_grp * LANES
    p_idx = pl.program_id(1)

    def online(s, values):
        m_old = m_ref[...]
        m_new = jnp.maximum(m_old, jnp.max(s, axis=-1, keepdims=True))
        alpha = jnp.exp2(m_old - m_new)
        p = jnp.exp2(s - m_new)
        m_ref[...] = m_new
        l_ref[...] = alpha * l_ref[...] + jnp.sum(p, axis=-1, keepdims=True)
        pb = p.astype(BF16)
        pv = None
        for (lo, hi), xv in values:
            term = _dot(pb[:, lo:hi], xv)
            pv = term if pv is None else pv + term
        acc_ref[...] = alpha * acc_ref[...] + pv

    @pl.when(p_idx == 0)
    def _():
        row = lax.broadcasted_iota(jnp.int32, (nq, LANES), 0)
        lane = lax.broadcasted_iota(jnp.int32, (nq, LANES), 1)
        q = q_ref[0]
        qall_ref[...] = jnp.concatenate([q[:, h * HEAD_DIM:(h + 1) * HEAD_DIM] for h in range(N_HEADS)],
                                        axis=0).astype(BF16)
        same_head = (lane & (N_HEADS - 1)) == (row >> _log2(t_new))
        mask_ref[...] = jnp.where(same_head, cq_ref[0] * LOG2E, -jnp.inf)
        m_ref[...] = jnp.full(m_ref.shape, -jnp.inf, F32)
        l_ref[...] = jnp.zeros(l_ref.shape, F32)
        acc_ref[...] = jnp.zeros(acc_ref.shape, F32)
        tail_ref[...] = jnp.zeros(tail_ref.shape, F32)
        pad = LANES - nq
        kn = jnp.concatenate([kn_ref[0], jnp.zeros((pad, HEAD_DIM), F32)], axis=0).astype(BF16)
        vn = jnp.concatenate([vn_ref[0], jnp.zeros((pad, HEAD_DIM), F32)], axis=0).astype(BF16)
        s = _dot_nt(qall_ref[...], kn) + (mask_ref[...] - ckn_ref[0] * LOG2E)
        s = jnp.where((lane >> _log2(N_HEADS)) <= (row & (t_new - 1)), s, -jnp.inf)
        online(s, [((0, LANES), vn)])

    qall = qall_ref[...]
    mask = mask_ref[...]
    tail = tail_ref[...]
    zeros8 = jnp.zeros((SUBLANES, LANES), F32)
    grp_row = lax.broadcasted_iota(jnp.int32, (n_grp, LANES), 0)
    s_parts = []
    for u in range(n_pb):
        lp = f_refs[u][0]
        hi, mid, lo = _split3(lp)
        l3 = jnp.concatenate([hi, mid, lo, zeros8], axis=0).astype(BF16)
        pin = _dot(l3, uin_ref[...])
        psum = _dot(l3, usum_ref[...])
        intra = pin[0:n_grp] + pin[n_grp:2 * n_grp] + pin[2 * n_grp:3 * n_grp]
        tot = psum[0:n_grp] + psum[n_grp:2 * n_grp] + psum[2 * n_grp:3 * n_grp]
        r = tot
        d = 1
        while d < n_grp:
            r = r + jnp.where(grp_row + d < n_grp, pltpu.roll(r, n_grp - d, 0), 0.0)
            d *= 2
        suf = (intra + (r - tot) + tail) * LOG2E
        tail = tail + r[0:1, :]
        sk = _dot_nt(qall, k_refs[u][0].astype(BF16))
        s_parts += [sk[:, g * LANES:(g + 1) * LANES] + (mask + suf[g:g + 1, :]) for g in range(n_grp)]
    tail_ref[...] = tail
    online(jnp.concatenate(s_parts, axis=1),
           [((u * page_w, (u + 1) * page_w), v_refs[u][0].astype(BF16)) for u in range(n_pb)])

    @pl.when(p_idx == pl.num_programs(1) - 1)
    def _():
        o = acc_ref[...] / l_ref[...]
        for h in range(N_HEADS):
            o_ref[0, :, h * HEAD_DIM:(h + 1) * HEAD_DIM] = o[h * t_new:(h + 1) * t_new, :]


def _suffix_matrices():
    idx = np.arange(LANES)
    key, head = idx // N_HEADS, idx % N_HEADS
    same = head[:, None] == head[None, :]
    later = same & (key[:, None] > key[None, :])
    return jnp.asarray(later, BF16), jnp.asarray(same, BF16)


def _attn_sample(page_ids, cache_k, cache_v, cache_logf, q, kn, vn, cq, ckn):
    db, n_pages = page_ids.shape
    page_rows = cache_k.shape[1]
    n_grp = page_rows // LANES
    t_new = q.shape[1]
    nq = N_HEADS * t_new
    assert t_new == SUBLANES and n_grp == SUBLANES and cache_logf.shape[1:] == (n_grp, LANES)
    n_pb = PAGES_PER_STEP if n_pages % PAGES_PER_STEP == 0 else 1
    later, same = _suffix_matrices()

    def page_map(u):
        def index_map(b, p, pt):
            return (pt[b * n_pages + (n_pages - 1 - (p * n_pb + u))], 0, 0)
        return index_map

    per_b = lambda r, w: pl.BlockSpec((1, r, w), lambda b, p, pt: (b, 0, 0))
    const = pl.BlockSpec((LANES, LANES), lambda b, p, pt: (0, 0))
    in_specs = ([pl.BlockSpec((1, page_rows, HEAD_DIM), page_map(u)) for u in range(n_pb)]
                + [pl.BlockSpec((1, page_rows, HEAD_DIM), page_map(u)) for u in range(n_pb)]
                + [pl.BlockSpec((1, n_grp, LANES), page_map(u)) for u in range(n_pb)]
                + [per_b(t_new, D_MODEL), per_b(nq, HEAD_DIM), per_b(nq, HEAD_DIM), per_b(nq, 1), per_b(1, LANES),
                   const, const])
    grid_spec = pltpu.PrefetchScalarGridSpec(
        num_scalar_prefetch=1,
        grid=(db, n_pages // n_pb),
        in_specs=in_specs,
        out_specs=pl.BlockSpec((1, t_new, D_MODEL), lambda b, p, pt: (b, 0, 0)),
        scratch_shapes=[pltpu.VMEM((nq, HEAD_DIM), BF16), pltpu.VMEM((nq, LANES), F32), pltpu.VMEM((nq, 1), F32),
                        pltpu.VMEM((nq, 1), F32), pltpu.VMEM((nq, HEAD_DIM), F32), pltpu.VMEM((1, LANES), F32)],
    )
    return pl.pallas_call(
        functools.partial(_attn_sample_kernel, n_pb=n_pb, n_grp=n_grp),
        grid_spec=grid_spec,
        out_shape=jax.ShapeDtypeStruct((db, t_new, D_MODEL), F32),
        compiler_params=pltpu.CompilerParams(dimension_semantics=("arbitrary", "arbitrary"),
                                             vmem_limit_bytes=VMEM_LIMIT),
        name="attn_sample",
    )(page_ids.reshape(-1), *([cache_k] * n_pb), *([cache_v] * n_pb), *([cache_logf] * n_pb),
      q, kn, vn, cq, ckn, later, same)


def _gelu_tanh(x):
    return 0.5 * x * (1.0 + jnp.tanh(np.float32(np.sqrt(2.0 / np.pi)) * (x + 0.044715 * (x * x * x))))


def _post_kernel(x_ref, og_ref, oa_ref, sga_ref, p_ref, hist_ref,
                 wo_ref, gpost_ref, gpre_ref, wup_ref, fcw_ref, fcb_ref, wdn_ref, gfpost_ref,
                 wple_ref, wpg_ref, gple_ref,
                 y_ref, hout_ref,
                 hh_ref, xg_ref, xv_ref, *, inter, tiles_per_seq):
    tm = x_ref.shape[0]
    nh = (FFN_CONV_W - 1) * inter
    hp = hh_ref.shape[0]
    i = pl.program_id(0)

    @pl.when(i % tiles_per_seq == 0)
    def _():
        if hp > nh:
            hh_ref[...] = jnp.zeros_like(hh_ref)
        hh_ref[hp - nh:hp, :] = hist_ref[0]

    merged = og_ref[...].astype(F32) + sga_ref[...].astype(F32) * oa_ref[...].astype(F32)
    x1 = x_ref[...] + _rms(_dot(merged.astype(BF16), wo_ref[...]), gpost_ref[...])
    hn = _rms(x1, gpre_ref[...]).astype(BF16)

    n_chunks = D_FF // FFN_CHUNK

    def up_conv(c):
        halves = []
        for xp_ref, base in ((xg_ref, 0), (xv_ref, D_FF)):
            cols = slice(base + c * FFN_CHUNK, base + (c + 1) * FFN_CHUNK)
            xp_ref[c, 0:hp, :] = hh_ref[:, cols]
            xp_ref[c, hp:hp + tm, :] = _dot(hn, wup_ref[:, cols])
            hc = fcb_ref[:, cols]
            for j in range(FFN_CONV_W):
                off = hp - (FFN_CONV_W - 1 - j) * inter
                hc = hc + fcw_ref[j:j + 1, cols] * xp_ref[c, off:off + tm, :]
            hh_ref[:, cols] = xp_ref[c, tm:tm + hp, :]
            halves.append(hc)
        return halves

    y2 = jnp.zeros((tm, D_MODEL), F32)
    nxt = up_conv(0)
    for c in range(n_chunks):
        cur = nxt
        if c + 1 < n_chunks:
            nxt = up_conv(c + 1)
        act = (_gelu_tanh(cur[0]) * cur[1]).astype(BF16)
        y2 = y2 + _dot(act, wdn_ref[c * FFN_CHUNK:(c + 1) * FFN_CHUNK, :])
    hout_ref[0] = hh_ref[hp - nh:hp, :]

    x2 = x1 + _rms(y2, gfpost_ref[...])
    e = _dot(p_ref[...].astype(BF16), wple_ref[...]) * _sigmoid(_dot(x2.astype(BF16), wpg_ref[...]))
    y_ref[...] = x2 + _rms(e, gple_ref[...])


def _post(x2, og, oa, sga, p2, hist, inter, seq_rows, wts):
    n = x2.shape[0]
    tm = min(ROW_TILE, seq_rows)
    tiles_per_seq = seq_rows // tm
    assert tiles_per_seq == 1 or inter == 1
    n_groups = n // seq_rows
    nh = (FFN_CONV_W - 1) * inter
    hp = _round_up(nh, SUBLANES)
    assert tm >= hp
    d_ple = p2.shape[1]
    row = lambda w: pl.BlockSpec((tm, w), lambda i: (i, 0))
    grp = pl.BlockSpec((1, nh, 2 * D_FF), lambda i: (i // tiles_per_seq, 0, 0))
    vec = _const_spec((1, D_MODEL))
    return pl.pallas_call(
        functools.partial(_post_kernel, inter=inter, tiles_per_seq=tiles_per_seq),
        grid=(n // tm,),
        in_specs=[row(D_MODEL), row(D_MODEL), row(D_MODEL), row(D_MODEL), row(d_ple), grp,
                  _const_spec((D_MODEL, D_MODEL)), vec, vec, _const_spec((D_MODEL, 2 * D_FF)),
                  _const_spec((FFN_CONV_W, 2 * D_FF)), _const_spec((1, 2 * D_FF)), _const_spec((D_FF, D_MODEL)), vec,
                  _const_spec((d_ple, D_MODEL)), _const_spec((D_MODEL, D_MODEL)), vec],
        out_specs=(row(D_MODEL), grp),
        out_shape=(jax.ShapeDtypeStruct((n, D_MODEL), F32),
                   jax.ShapeDtypeStruct((n_groups, nh, 2 * D_FF), F32)),
        scratch_shapes=[pltpu.VMEM((hp, 2 * D_FF), F32), pltpu.VMEM((D_FF // FFN_CHUNK, hp + tm, FFN_CHUNK), F32),
                        pltpu.VMEM((D_FF // FFN_CHUNK, hp + tm, FFN_CHUNK), F32)],
        compiler_params=pltpu.CompilerParams(dimension_semantics=("arbitrary",), vmem_limit_bytes=VMEM_LIMIT),
        name="post_ffn",
    )(x2, og, oa, sga, p2, hist, wts["w_out"], wts["g_mix_post"], wts["g_ffn_pre"], wts["w_up"], wts["ffn_cw"],
      wts["ffn_cb"], wts["w_down"], wts["g_ffn_post"], wts["w_ple"], wts["w_ple_gate"], wts["g_ple_post"])


def _prep_weights(l, norm_mix_pre, w_in, b_forget, lru_conv_w, lru_conv_b, lru_wa, lru_ba, lru_wx, lru_bx,
                  lru_lambda, w_out, norm_mix_post, norm_ffn_pre, w_up, ffn_conv_w, ffn_conv_b, w_down,
                  norm_ffn_post, w_ple, w_ple_gate, norm_ple_post):
    d = D_MODEL
    wi = w_in[l]
    offs = np.cumsum([0, d, d, d, d, N_HEADS, d, d])
    piece = lambda k: wi[:, int(offs[k]):int(offs[k + 1])].astype(BF16)
    vec = lambda a: a[l].reshape(1, -1).astype(F32)
    per_tile = MXU_DIM // LRU_BLOCK
    n_tiles = LRU_HEADS // per_tile

    def block_diag(w):
        w4 = w.reshape(n_tiles, per_tile, LRU_BLOCK, LRU_BLOCK)
        eye = jnp.eye(per_tile, dtype=w.dtype)
        return jnp.einsum("thij,hg->thigj", w4, eye).reshape(n_tiles, MXU_DIM, MXU_DIM)

    return {
        "g_mix_pre": vec(norm_mix_pre),
        "w_lru": piece(0), "w_q": piece(1), "w_k": piece(2), "w_v": piece(3),
        "w_f": jnp.pad(piece(4), ((0, 0), (0, LANES - N_HEADS))),
        "w_gl": piece(5), "w_ga": piece(6),
        "b_f": jnp.pad(vec(b_forget), ((0, 0), (0, LANES - N_HEADS))),
        "lru_cw": lru_conv_w[l].astype(F32), "lru_cb": vec(lru_conv_b),
        "lru_wab": jnp.concatenate([block_diag(lru_wa[l]), block_diag(lru_wx[l])], axis=-1).astype(BF16),
        "lru_ba": vec(lru_ba), "lru_bx": vec(lru_bx), "lru_lam": vec(lru_lambda),
        "w_out": w_out[l].astype(BF16), "g_mix_post": vec(norm_mix_post), "g_ffn_pre": vec(norm_ffn_pre),
        "w_up": w_up[l].astype(BF16), "ffn_cw": ffn_conv_w[l].astype(F32), "ffn_cb": vec(ffn_conv_b),
        "w_down": w_down[l].astype(BF16), "g_ffn_post": vec(norm_ffn_post),
        "w_ple": w_ple[l].astype(BF16), "w_ple_gate": w_ple_gate[l].astype(BF16), "g_ple_post": vec(norm_ple_post),
    }


def _prompt_layer(x, p_l, wts):
    b, s, d = x.shape
    n = b * s
    qa, ka, kf, vf, vt, lf, _, sga, og, h_last, lru_hist = _in_proj(
        x.reshape(n, d), jnp.zeros((b, LRU_CONV_W - 1, d), F32), jnp.zeros((b, 1, d), F32), 1, s, wts)
    oa = _attn_prompt(qa.reshape(b, s, -1), ka.reshape(b, s, -1), vt.reshape(b, N_HEADS, HEAD_DIM, s))
    y, ffn_hist = _post(x.reshape(n, d), og, oa.reshape(n, d), sga, p_l.reshape(n, -1),
                        jnp.zeros((b, FFN_CONV_W - 1, 2 * D_FF), F32), 1, s, wts)
    state = (kf.reshape(b, s, N_HEADS, HEAD_DIM), vf.reshape(b, s, N_HEADS, HEAD_DIM),
             lf[:, :N_HEADS].reshape(b, s, N_HEADS), h_last.reshape(b, d), lru_hist, ffn_hist)
    return y.reshape(b, s, d), state


def _sample_layer(x, p_l, cache_k, cache_v, cache_logf, page_ids, lru_buf, lru_h0, ffn_buf, wts):
    db, t, d = x.shape
    n = db * t
    tmaj = lambda a: jnp.swapaxes(a, 0, 1).reshape(a.shape[0] * a.shape[1], -1)
    bmaj = lambda a: jnp.swapaxes(a.reshape(-1, db, a.shape[-1]), 0, 1)

    xt = tmaj(x)
    qa, _, kf, vf, _, lf, c, sga, og, h_last, lru_hist = _in_proj(xt, tmaj(lru_buf)[None], lru_h0[None], db, n, wts)

    k_b, v_b, c_b = bmaj(kf), bmaj(vf), bmaj(c)[..., :N_HEADS]
    q_b = bmaj(qa).astype(F32).reshape(db, t, N_HEADS, AUG_DIM)[..., :HEAD_DIM].reshape(db, t, d)
    cq = jnp.swapaxes(c_b, 1, 2).reshape(db, N_HEADS * t, 1)
    ckn = jnp.pad(c_b.reshape(db, 1, t * N_HEADS), ((0, 0), (0, 0), (0, LANES - t * N_HEADS)))
    oa_b = _attn_sample(page_ids, cache_k, cache_v, cache_logf, q_b,
                        k_b.reshape(db, t * N_HEADS, HEAD_DIM), v_b.reshape(db, t * N_HEADS, HEAD_DIM), cq, ckn)

    y, ffn_hist = _post(xt, og, tmaj(oa_b), sga, tmaj(p_l), tmaj(ffn_buf)[None], db, n, wts)
    state = (k_b.reshape(db, t, N_HEADS, HEAD_DIM), v_b.reshape(db, t, N_HEADS, HEAD_DIM),
             bmaj(lf)[..., :N_HEADS], h_last.reshape(db, d),
             bmaj(lru_hist.reshape(-1, d)), bmaj(ffn_hist.reshape(-1, 2 * D_FF)))
    return bmaj(y), state


def kernel(x_prompt, x_sample, cache_k, cache_v, cache_logf, state_lru_h, state_lru_conv, state_ffn_conv, page_table, p_prompt, p_sample, norm_mix_pre, w_in, b_forget, lru_conv_w, lru_conv_b, lru_wa, lru_ba, lru_wx, lru_bx, lru_lambda, w_out, norm_mix_post, norm_ffn_pre, w_up, ffn_conv_w, ffn_conv_b, w_down, norm_ffn_post, w_ple, w_ple_gate, norm_ple_post):
    depth, n_pool, page = cache_k.shape[:3]
    ck = cache_k.reshape(depth * n_pool, page * N_HEADS, HEAD_DIM)
    cv = cache_v.reshape(depth * n_pool, page * N_HEADS, HEAD_DIM)
    cf = cache_logf.reshape(depth * n_pool, page * N_HEADS // LANES, LANES)
    y_p, y_s = x_prompt, x_sample
    outs_p, outs_s = [], []
    for l in range(depth):
        wts = _prep_weights(l, norm_mix_pre, w_in, b_forget, lru_conv_w, lru_conv_b, lru_wa, lru_ba, lru_wx, lru_bx,
                            lru_lambda, w_out, norm_mix_post, norm_ffn_pre, w_up, ffn_conv_w, ffn_conv_b, w_down,
                            norm_ffn_post, w_ple, w_ple_gate, norm_ple_post)
        y_p, st_p = _prompt_layer(y_p, p_prompt[l], wts)
        y_s, st_s = _sample_layer(y_s, p_sample[l], ck, cv, cf, page_table + l * n_pool,
                                  state_lru_conv[l], state_lru_h[l], state_ffn_conv[l], wts)
        outs_p.append(st_p)
        outs_s.append(st_s)
    stack = lambda outs, k: jnp.stack([o[k] for o in outs])
    res = [y_p, y_s]
    for k in range(6):
        res.append(stack(outs_p, k))
        res.append(stack(outs_s, k))
    return tuple(res)
```

```python
import functools

import numpy as np
import jax
import jax.numpy as jnp
from jax import lax
from jax.experimental import pallas as pl
from jax.experimental.pallas import tpu as pltpu

D_MODEL = 1024
N_HEADS = 8
HEAD_DIM = 128
ATT_SCALE = HEAD_DIM ** -0.5
LRU_HEADS = 16
LRU_BLOCK = 64
LRU_CONV_W = 4
LRU_C = 8.0
D_FF = 3 * D_MODEL
FFN_CONV_W = 3
RMS_EPS = 1e-6

LANES = 128
SUBLANES = 8
MXU_DIM = 256
AUG_DIM = 2 * HEAD_DIM
VMEM_LIMIT = 56 * 1024 * 1024

ROW_TILE = 256
ATT_TILE = 2048
ATT_KEY_CHUNK = MXU_DIM
FFN_CHUNK = 1024
PAGES_PER_STEP = 16
LOG2E = float(np.log2(np.e))

F32 = jnp.float32
BF16 = jnp.bfloat16


def _round_up(n, m):
    return (n + m - 1) // m * m


def _log2(n):
    assert n > 0 and n & (n - 1) == 0
    return n.bit_length() - 1


def _rms(x, g):
    ms = jnp.mean(x * x, axis=-1, keepdims=True)
    return (x * lax.rsqrt(ms + RMS_EPS)) * g


def _sigmoid(x):
    return 0.5 * jnp.tanh(0.5 * x) + 0.5


def _softplus(x):
    return jnp.maximum(x, 0.0) + jnp.log1p(jnp.exp(-jnp.abs(x)))


def _dot(a, b):
    return jnp.dot(a, b, preferred_element_type=F32)


def _dot_nt(a, b):
    return lax.dot_general(a, b, (((1,), (1,)), ((), ())), preferred_element_type=F32)


def _split3(c):
    hi = c.astype(BF16).astype(F32)
    r1 = c - hi
    mid = r1.astype(BF16).astype(F32)
    lo = (r1 - mid).astype(BF16).astype(F32)
    return hi, mid, lo


def _const_spec(shape):
    nd = len(shape)
    return pl.BlockSpec(shape, lambda *_: (0,) * nd, pipeline_mode=pl.Buffered(1))


def _in_proj_kernel(x_ref, g_ref, wl_ref, wq_ref, wk_ref, wv_ref, wf_ref, wgl_ref, wga_ref, bf_ref,
                    pq_ref, pk_ref, hist_ref, h0_ref, cw_ref, cb_ref, wab_ref, ba_ref, bx_ref, lam_ref,
                    qa_ref, ka_ref, kf_ref, vf_ref, vt_ref, lf_ref, c_ref, sga_ref, og_ref, hl_ref, hout_ref,
                    carry_ref, xp_ref, hcar_ref, *, inter, tiles_per_seq, steps):
    tm = x_ref.shape[0]
    i = pl.program_id(0)
    ub = _rms(x_ref[...], g_ref[...]).astype(BF16)

    lru = _lru_tile(ub, wl_ref, wgl_ref, i, hist_ref, h0_ref, cw_ref, cb_ref, wab_ref, ba_ref, bx_ref, lam_ref,
                    og_ref, hl_ref, hout_ref, xp_ref, hcar_ref, inter=inter, tiles_per_seq=tiles_per_seq, steps=steps)
    n_blk = D_MODEL // MXU_DIM
    kbs, qbs = [], []

    def piece(kind, blk):
        cols = slice(blk * MXU_DIM, (blk + 1) * MXU_DIM)
        if kind == "k":
            kblk = _dot(ub, wk_ref[:, cols])
            kf_ref[:, cols] = kblk
            kbs.append(kblk.astype(BF16))
        elif kind == "v":
            vblk = _dot(ub, wv_ref[:, cols])
            vf_ref[:, cols] = vblk
            vt_ref[0, cols, :] = vblk.T.astype(BF16)
        elif kind == "ga":
            sga_ref[:, cols] = _sigmoid(_dot(ub, wga_ref[:, cols])).astype(BF16)
        else:
            qbs.append((_dot(ub, wq_ref[:, cols]) * (ATT_SCALE * LOG2E)).astype(BF16))

    pieces = [(kind, blk) for kind in ("k", "v", "ga", "q") for blk in range(n_blk)]
    for stage in lru:
        if pieces:
            piece(*pieces.pop(0))
    while pieces:
        piece(*pieces.pop(0))
    qs = jnp.concatenate(qbs, axis=1)
    kb = jnp.concatenate(kbs, axis=1)

    lf = -_softplus(-(_dot(ub, wf_ref[...]) + bf_ref[...]))
    lf_ref[...] = lf
    row = lax.broadcasted_iota(jnp.int32, (tm, LANES), 0)
    c = lf
    d = inter
    while d < tm:
        c = c + jnp.where(row >= d, pltpu.roll(c, d, 0), 0.0)
        d *= 2
    if tiles_per_seq > 1:
        @pl.when(i % tiles_per_seq == 0)
        def _():
            carry_ref[...] = jnp.zeros_like(carry_ref)
        c = c + carry_ref[0:1, :]
        carry_ref[0:1, :] = c[tm - 1:tm, :]
    c_ref[...] = c

    hi, mid, lo = _split3(c * LOG2E)
    lane = lax.broadcasted_iota(jnp.int32, (tm, LANES), 1)
    cp = jnp.where(lane < N_HEADS, hi,
                   jnp.where(lane < 2 * N_HEADS, pltpu.roll(mid, N_HEADS, 1),
                             jnp.where(lane < 3 * N_HEADS, pltpu.roll(lo, 2 * N_HEADS, 1),
                                       jnp.where(lane == 3 * N_HEADS, 1.0, 0.0)))).astype(BF16)
    qx = _dot(cp, pq_ref[...]).astype(BF16)
    kx = _dot(cp, pk_ref[...]).astype(BF16)
    for h in range(N_HEADS):
        src = slice(h * HEAD_DIM, (h + 1) * HEAD_DIM)
        qa_ref[:, h * AUG_DIM:h * AUG_DIM + HEAD_DIM] = qs[:, src]
        qa_ref[:, h * AUG_DIM + HEAD_DIM:(h + 1) * AUG_DIM] = qx[:, src]
        ka_ref[:, h * AUG_DIM:h * AUG_DIM + HEAD_DIM] = kb[:, src]
        ka_ref[:, h * AUG_DIM + HEAD_DIM:(h + 1) * AUG_DIM] = kx[:, src]


def _bias_scatter_matrices():
    pq = np.zeros((LANES, D_MODEL), np.float32)
    pk = np.zeros((LANES, D_MODEL), np.float32)
    for h in range(N_HEADS):
        base = h * HEAD_DIM
        for part in range(3):
            pq[part * N_HEADS + h, base + part] = 1.0
            pq[3 * N_HEADS, base + 3 + part] = 1.0
            pk[3 * N_HEADS, base + part] = 1.0
            pk[part * N_HEADS + h, base + 3 + part] = -1.0
    return jnp.asarray(pq, BF16), jnp.asarray(pk, BF16)


def _in_proj(x2, hist, h0, inter, seq_rows, wts):
    n = x2.shape[0]
    tm = min(ROW_TILE, seq_rows)
    assert n % tm == 0 and seq_rows % tm == 0
    tiles_per_seq = seq_rows // tm
    assert tiles_per_seq == 1 or inter == 1
    n_groups = n // seq_rows
    nh = (LRU_CONV_W - 1) * inter
    assert tm >= nh
    hp = _round_up(nh, SUBLANES)
    pq, pk = _bias_scatter_matrices()
    row = lambda w: pl.BlockSpec((tm, w), lambda i: (i, 0))
    grp = lambda r: pl.BlockSpec((1, r, D_MODEL), lambda i: (i // tiles_per_seq, 0, 0))
    sq = _const_spec((D_MODEL, D_MODEL))
    vec = _const_spec((1, D_MODEL))
    out_shape = (
        jax.ShapeDtypeStruct((n, N_HEADS * AUG_DIM), BF16),
        jax.ShapeDtypeStruct((n, N_HEADS * AUG_DIM), BF16),
        jax.ShapeDtypeStruct((n, D_MODEL), F32),
        jax.ShapeDtypeStruct((n, D_MODEL), F32),
        jax.ShapeDtypeStruct((n_groups, D_MODEL, seq_rows), BF16),
        jax.ShapeDtypeStruct((n, LANES), F32),
        jax.ShapeDtypeStruct((n, LANES), F32),
        jax.ShapeDtypeStruct((n, D_MODEL), BF16),
        jax.ShapeDtypeStruct((n, D_MODEL), BF16),
        jax.ShapeDtypeStruct((n_groups, inter, D_MODEL), F32),
        jax.ShapeDtypeStruct((n_groups, nh, D_MODEL), F32),
    )
    vt_spec = pl.BlockSpec((1, D_MODEL, tm), lambda i: (i // tiles_per_seq, 0, i % tiles_per_seq))
    return pl.pallas_call(
        functools.partial(_in_proj_kernel, inter=inter, tiles_per_seq=tiles_per_seq, steps=seq_rows // inter),
        grid=(n // tm,),
        in_specs=[row(D_MODEL), vec, sq, sq, sq, sq,
                  _const_spec((D_MODEL, LANES)), sq, sq, _const_spec((1, LANES)),
                  _const_spec((LANES, D_MODEL)), _const_spec((LANES, D_MODEL)),
                  grp(nh), grp(inter), _const_spec((LRU_CONV_W, D_MODEL)), vec,
                  _const_spec((D_MODEL // MXU_DIM, MXU_DIM, 2 * MXU_DIM)), vec, vec, vec],
        out_specs=(row(N_HEADS * AUG_DIM), row(N_HEADS * AUG_DIM), row(D_MODEL), row(D_MODEL),
                   vt_spec, row(LANES), row(LANES), row(D_MODEL), row(D_MODEL), grp(inter), grp(nh)),
        out_shape=out_shape,
        scratch_shapes=[pltpu.VMEM((SUBLANES, LANES), F32),
                        pltpu.VMEM((hp + tm, D_MODEL), F32), pltpu.VMEM((max(inter, SUBLANES), D_MODEL), F32)],
        compiler_params=pltpu.CompilerParams(dimension_semantics=("arbitrary",), vmem_limit_bytes=VMEM_LIMIT),
        name="in_proj",
    )(x2, wts["g_mix_pre"], wts["w_lru"], wts["w_q"], wts["w_k"], wts["w_v"], wts["w_f"], wts["w_gl"],
      wts["w_ga"], wts["b_f"], pq, pk, hist, h0, wts["lru_cw"], wts["lru_cb"], wts["lru_wab"], wts["lru_ba"],
      wts["lru_bx"], wts["lru_lam"])


def _scan8(a, b):
    row = lax.broadcasted_iota(jnp.int32, a.shape, 0)
    for d in (1, 2, 4):
        keep = row >= d
        a_s = pltpu.roll(a, d, 0)
        b_s = pltpu.roll(b, d, 0)
        b = jnp.where(keep, a * b_s + b, b)
        a = jnp.where(keep, a * a_s, a)
    return a, b


def _lru_tile(ub, wl_ref, wgl_ref, i, hist_ref, h0_ref, cw_ref, cb_ref, wab_ref, ba_ref, bx_ref, lam_ref,
              og_ref, hl_ref, hout_ref, xp_ref, hcar_ref, *, inter, tiles_per_seq, steps):
    tt = ub.shape[0]
    nh = (LRU_CONV_W - 1) * inter
    hp = xp_ref.shape[0] - tt

    @pl.when(i % tiles_per_seq == 0)
    def _():
        xp_ref[hp - nh:hp, :] = hist_ref[0]
        hcar_ref[0:inter, :] = h0_ref[0]

    sp = _softplus(-lam_ref[...])
    for blk in range(D_MODEL // MXU_DIM):
        cols = slice(blk * MXU_DIM, (blk + 1) * MXU_DIM)
        xp_ref[hp:hp + tt, cols] = _dot(ub, wl_ref[:, cols])
        xc = cb_ref[:, cols]
        for j in range(LRU_CONV_W):
            off = hp - (LRU_CONV_W - 1 - j) * inter
            xc = xc + cw_ref[j:j + 1, cols] * xp_ref[off:off + tt, cols]
        xp_ref[hp - nh:hp, cols] = xp_ref[hp + tt - nh:hp + tt, cols]
        hout_ref[0, :, cols] = xp_ref[hp - nh:hp, cols]
        yield

        gates = _dot(xc.astype(BF16), wab_ref[blk])
        r = _sigmoid(gates[:, :MXU_DIM] + ba_ref[:, cols])
        ig = _sigmoid(gates[:, MXU_DIM:] + bx_ref[:, cols])
        log_a = (-LRU_C * r) * sp[:, cols]
        a = jnp.exp(log_a)
        th = jnp.tanh(log_a)
        b = jnp.sqrt((-2.0 * th) / (1.0 - th)) * (ig * xc)
        yield
        hs = []
        if inter == 1:
            carry = jnp.broadcast_to(hcar_ref[0:1, cols], (SUBLANES, MXU_DIM))
            for g in range(tt // SUBLANES):
                rows = slice(g * SUBLANES, (g + 1) * SUBLANES)
                a8, b8 = _scan8(a[rows, :], b[rows, :])
                h8 = a8 * carry + b8
                hs.append(h8)
                carry = jnp.broadcast_to(h8[SUBLANES - 1:SUBLANES, :], (SUBLANES, MXU_DIM))
            hcar_ref[0:1, cols] = carry[0:1, :]
        else:
            h = hcar_ref[0:inter, cols]
            for t in range(steps):
                rows = slice(t * inter, (t + 1) * inter)
                h = a[rows, :] * h + b[rows, :]
                hs.append(h)
            hcar_ref[0:inter, cols] = h
        sgl = _sigmoid(_dot(ub, wgl_ref[:, cols]))
        og_ref[:, cols] = (sgl * jnp.concatenate(hs, axis=0)).astype(BF16)
        yield
    hl_ref[0] = hcar_ref[0:inter, :]


def _attn_prompt_kernel(q_ref, k_ref, vt_ref, o_ref, sa_ref, sb_ref, xa_ref, xb_ref, m_ref, l_ref, acc_ref):
    tq = q_ref.shape[1]
    tk = sa_ref.shape[0]
    kc = min(ATT_KEY_CHUNK, tk)
    i = pl.program_id(2)

    def put_scores(j, dst, mx=None, lo=0):
        k0 = pl.multiple_of(j * tk, tk)
        sc = _dot_nt(k_ref[0, pl.ds(k0, tk), :], q_ref[0, lo:, :])
        dst[:, lo:] = sc
        if mx is not None:
            mx[...] = jnp.max(sc, axis=0, keepdims=True)

    def update(j, src, mx=None, lo=0, diagonal=False):
        k0 = pl.multiple_of(j * tk, tk)
        m = m_ref[:, lo:]
        if diagonal:
            kpos = lax.broadcasted_iota(jnp.int32, (tk, tk), 0)
            qpos = lax.broadcasted_iota(jnp.int32, (tk, tk), 1)
            src[:, lo:lo + tk] = jnp.where(kpos <= qpos, src[:, lo:lo + tk], -jnp.inf)
            m_new = m
            for c in range(tk // kc):
                m_new = jnp.maximum(m_new, jnp.max(src[c * kc:(c + 1) * kc, lo:], axis=0, keepdims=True))
        else:
            m_new = jnp.maximum(m, mx[...])
        alpha = jnp.exp2(m - m_new)
        m_ref[:, lo:] = m_new
        l = alpha * l_ref[:, lo:]
        acc = alpha * acc_ref[:, lo:]
        for c in range(tk // kc):
            p = jnp.exp2(src[c * kc:(c + 1) * kc, lo:] - m_new)
            l = l + jnp.sum(p, axis=0, keepdims=True)
            acc = acc + _dot(vt_ref[0, 0, :, pl.ds(k0 + c * kc, kc)], p.astype(BF16))
        l_ref[:, lo:] = l
        acc_ref[:, lo:] = acc

    m_ref[...] = jnp.full(m_ref.shape, -jnp.inf, F32)
    l_ref[...] = jnp.zeros(l_ref.shape, F32)
    acc_ref[...] = jnp.zeros(acc_ref.shape, F32)
    put_scores(0, sa_ref, xa_ref)

    def pair(jj, carry):
        j = 2 * jj
        put_scores(j + 1, sb_ref, xb_ref)
        update(j, sa_ref, xa_ref)
        put_scores(j + 2, sa_ref, xa_ref)
        update(j + 1, sb_ref, xb_ref)
        return carry

    lax.fori_loop(0, i, pair, 0)
    put_scores(2 * i + 1, sb_ref, lo=tk)
    update(2 * i, sa_ref, diagonal=True)
    update(2 * i + 1, sb_ref, lo=tk, diagonal=True)
    o_ref[0] = (acc_ref[...] / l_ref[...]).T.astype(o_ref.dtype)


def _attn_prompt(qa, ka, vt):
    b, s, _ = qa.shape
    tq = min(ATT_TILE, s)
    tk = tq // 2
    assert s % tq == 0 and tk % LANES == 0
    return pl.pallas_call(
        _attn_prompt_kernel,
        grid=(b, N_HEADS, s // tq),
        in_specs=[pl.BlockSpec((1, tq, AUG_DIM), lambda b_, h, i: (b_, i, h)),
                  pl.BlockSpec((1, s, AUG_DIM), lambda b_, h, i: (b_, 0, h)),
                  pl.BlockSpec((1, 1, HEAD_DIM, s), lambda b_, h, i: (b_, h, 0, 0))],
        out_specs=pl.BlockSpec((1, tq, HEAD_DIM), lambda b_, h, i: (b_, i, h)),
        out_shape=jax.ShapeDtypeStruct((b, s, D_MODEL), BF16),
        scratch_shapes=[pltpu.VMEM((tk, tq), F32), pltpu.VMEM((tk, tq), F32), pltpu.VMEM((1, tq), F32),
                        pltpu.VMEM((1, tq), F32), pltpu.VMEM((1, tq), F32), pltpu.VMEM((1, tq), F32),
                        pltpu.VMEM((HEAD_DIM, tq), F32)],
        compiler_params=pltpu.CompilerParams(dimension_semantics=("arbitrary", "arbitrary", "arbitrary"),
                                             vmem_limit_bytes=VMEM_LIMIT),
        name="attn_prompt",
    )(qa, ka, vt)


KEYS_PER_GROUP = LANES // N_HEADS


def _attn_sample_kernel(pt_ref, *refs, n_pb, n_grp):
    k_refs = refs[0:n_pb]
    v_refs = refs[n_pb:2 * n_pb]
    f_refs = refs[2 * n_pb:3 * n_pb]
    (q_ref, kn_ref, vn_ref, cq_ref, ckn_ref, uin_ref, usum_ref, o_ref,
     qall_ref, mask_ref, m_ref, l_ref, acc_ref, tail_ref) = refs[3 * n_pb:]
    del pt_ref
    t_new = q_ref.shape[1]
    nq = N_HEADS * t_new
    page_w = n_grp * LANES
    p_idx = pl.program_id(1)

    def online(s, values):
        m_old = m_ref[...]
        m_new = jnp.maximum(m_old, jnp.max(s, axis=-1, keepdims=True))
        alpha = jnp.exp2(m_old - m_new)
        p = jnp.exp2(s - m_new)
        m_ref[...] = m_new
        l_ref[...] = alpha * l_ref[...] + jnp.sum(p, axis=-1, keepdims=True)
        pb = p.astype(BF16)
        pv = None
        for (lo, hi), xv in values:
            term = _dot(pb[:, lo:hi], xv)
            pv = term if pv is None else pv + term
        acc_ref[...] = alpha * acc_ref[...] + pv

    @pl.when(p_idx == 0)
    def _():
        row = lax.broadcasted_iota(jnp.int32, (nq, LANES), 0)
        lane = lax.broadcasted_iota(jnp.int32, (nq, LANES), 1)
        q = q_ref[0]
        qall_ref[...] = jnp.concatenate([q[:, h * HEAD_DIM:(h + 1) * HEAD_DIM] for h in range(N_HEADS)],
                                        axis=0).astype(BF16)
        same_head = (lane & (N_HEADS - 1)) == (row >> _log2(t_new))
        mask_ref[...] = jnp.where(same_head, cq_ref[0] * LOG2E, -jnp.inf)
        m_ref[...] = jnp.full(m_ref.shape, -jnp.inf, F32)
        l_ref[...] = jnp.zeros(l_ref.shape, F32)
        acc_ref[...] = jnp.zeros(acc_ref.shape, F32)
        tail_ref[...] = jnp.zeros(tail_ref.shape, F32)
        pad = LANES - nq
        kn = jnp.concatenate([kn_ref[0], jnp.zeros((pad, HEAD_DIM), F32)], axis=0).astype(BF16)
        vn = jnp.concatenate([vn_ref[0], jnp.zeros((pad, HEAD_DIM), F32)], axis=0).astype(BF16)
        s = _dot_nt(qall_ref[...], kn) + (mask_ref[...] - ckn_ref[0] * LOG2E)
        s = jnp.where((lane >> _log2(N_HEADS)) <= (row & (t_new - 1)), s, -jnp.inf)
        online(s, [((0, LANES), vn)])

    qall = qall_ref[...]
    mask = mask_ref[...]
    tail = tail_ref[...]
    zeros8 = jnp.zeros((SUBLANES, LANES), F32)
    grp_row = lax.broadcasted_iota(jnp.int32, (n_grp, LANES), 0)
    s_parts = []
    for u in range(n_pb):
        lp = f_refs[u][0]
        hi, mid, lo = _split3(lp)
        l3 = jnp.concatenate([hi, mid, lo, zeros8], axis=0).astype(BF16)
        pin = _dot(l3, uin_ref[...])
        psum = _dot(l3, usum_ref[...])
        intra = pin[0:n_grp] + pin[n_grp:2 * n_grp] + pin[2 * n_grp:3 * n_grp]
        tot = psum[0:n_grp] + psum[n_grp:2 * n_grp] + psum[2 * n_grp:3 * n_grp]
        r = tot
        d = 1
        while d < n_grp:
            r = r + jnp.where(grp_row + d < n_grp, pltpu.roll(r, n_grp - d, 0), 0.0)
            d *= 2
        suf = (intra + (r - tot) + tail) * LOG2E
        tail = tail + r[0:1, :]
        sk = _dot_nt(qall, k_refs[u][0].astype(BF16))
        s_parts += [sk[:, g * LANES:(g + 1) * LANES] + (mask + suf[g:g + 1, :]) for g in range(n_grp)]
    tail_ref[...] = tail
    online(jnp.concatenate(s_parts, axis=1),
           [((u * page_w, (u + 1) * page_w), v_refs[u][0].astype(BF16)) for u in range(n_pb)])

    @pl.when(p_idx == pl.num_programs(1) - 1)
    def _():
        o = acc_ref[...] / l_ref[...]
        for h in range(N_HEADS):
            o_ref[0, :, h * HEAD_DIM:(h + 1) * HEAD_DIM] = o[h * t_new:(h + 1) * t_new, :]


def _suffix_matrices():
    idx = np.arange(LANES)
    key, head = idx // N_HEADS, idx % N_HEADS
    same = head[:, None] == head[None, :]
    later = same & (key[:, None] > key[None, :])
    return jnp.asarray(later, BF16), jnp.asarray(same, BF16)


def _attn_sample(page_ids, cache_k, cache_v, cache_logf, q, kn, vn, cq, ckn):
    db, n_pages = page_ids.shape
    page_rows = cache_k.shape[1]
    n_grp = page_rows // LANES
    t_new = q.shape[1]
    nq = N_HEADS * t_new
    assert t_new == SUBLANES and n_grp == SUBLANES and cache_logf.shape[1:] == (n_grp, LANES)
    n_pb = PAGES_PER_STEP if n_pages % PAGES_PER_STEP == 0 else 1
    later, same = _suffix_matrices()

    def page_map(u):
        def index_map(b, p, pt):
            return (pt[b * n_pages + (n_pages - 1 - (p * n_pb + u))], 0, 0)
        return index_map

    per_b = lambda r, w: pl.BlockSpec((1, r, w), lambda b, p, pt: (b, 0, 0))
    const = pl.BlockSpec((LANES, LANES), lambda b, p, pt: (0, 0))
    in_specs = ([pl.BlockSpec((1, page_rows, HEAD_DIM), page_map(u)) for u in range(n_pb)]
                + [pl.BlockSpec((1, page_rows, HEAD_DIM), page_map(u)) for u in range(n_pb)]
                + [pl.BlockSpec((1, n_grp, LANES), page_map(u)) for u in range(n_pb)]
                + [per_b(t_new, D_MODEL), per_b(nq, HEAD_DIM), per_b(nq, HEAD_DIM), per_b(nq, 1), per_b(1, LANES),
                   const, const])
    grid_spec = pltpu.PrefetchScalarGridSpec(
        num_scalar_prefetch=1,
        grid=(db, n_pages // n_pb),
        in_specs=in_specs,
        out_specs=pl.BlockSpec((1, t_new, D_MODEL), lambda b, p, pt: (b, 0, 0)),
        scratch_shapes=[pltpu.VMEM((nq, HEAD_DIM), BF16), pltpu.VMEM((nq, LANES), F32), pltpu.VMEM((nq, 1), F32),
                        pltpu.VMEM((nq, 1), F32), pltpu.VMEM((nq, HEAD_DIM), F32), pltpu.VMEM((1, LANES), F32)],
    )
    return pl.pallas_call(
        functools.partial(_attn_sample_kernel, n_pb=n_pb, n_grp=n_grp),
        grid_spec=grid_spec,
        out_shape=jax.ShapeDtypeStruct((db, t_new, D_MODEL), F32),
        compiler_params=pltpu.CompilerParams(dimension_semantics=("arbitrary", "arbitrary"),
                                             vmem_limit_bytes=VMEM_LIMIT),
        name="attn_sample",
    )(page_ids.reshape(-1), *([cache_k] * n_pb), *([cache_v] * n_pb), *([cache_logf] * n_pb),
      q, kn, vn, cq, ckn, later, same)


def _gelu_tanh(x):
    return 0.5 * x * (1.0 + jnp.tanh(np.float32(np.sqrt(2.0 / np.pi)) * (x + 0.044715 * (x * x * x))))


def _post_kernel(x_ref, og_ref, oa_ref, sga_ref, p_ref, hist_ref,
                 wo_ref, gpost_ref, gpre_ref, wup_ref, fcw_ref, fcb_ref, wdn_ref, gfpost_ref,
                 wple_ref, wpg_ref, gple_ref,
                 y_ref, hout_ref,
                 hh_ref, xg_ref, xv_ref, *, inter, tiles_per_seq):
    tm = x_ref.shape[0]
    nh = (FFN_CONV_W - 1) * inter
    hp = hh_ref.shape[0]
    i = pl.program_id(0)

    @pl.when(i % tiles_per_seq == 0)
    def _():
        if hp > nh:
            hh_ref[...] = jnp.zeros_like(hh_ref)
        hh_ref[hp - nh:hp, :] = hist_ref[0]

    merged = og_ref[...].astype(F32) + sga_ref[...].astype(F32) * oa_ref[...].astype(F32)
    x1 = x_ref[...] + _rms(_dot(merged.astype(BF16), wo_ref[...]), gpost_ref[...])
    hn = _rms(x1, gpre_ref[...]).astype(BF16)

    n_chunks = D_FF // FFN_CHUNK

    def up_conv(c):
        halves = []
        for xp_ref, base in ((xg_ref, 0), (xv_ref, D_FF)):
            cols = slice(base + c * FFN_CHUNK, base + (c + 1) * FFN_CHUNK)
            xp_ref[c, 0:hp, :] = hh_ref[:, cols]
            xp_ref[c, hp:hp + tm, :] = _dot(hn, wup_ref[:, cols])
            hc = fcb_ref[:, cols]
            for j in range(FFN_CONV_W):
                off = hp - (FFN_CONV_W - 1 - j) * inter
                hc = hc + fcw_ref[j:j + 1, cols] * xp_ref[c, off:off + tm, :]
            hh_ref[:, cols] = xp_ref[c, tm:tm + hp, :]
            halves.append(hc)
        return halves

    y2 = jnp.zeros((tm, D_MODEL), F32)
    nxt = up_conv(0)
    for c in range(n_chunks):
        cur = nxt
        if c + 1 < n_chunks:
            nxt = up_conv(c + 1)
        act = (_gelu_tanh(cur[0]) * cur[1]).astype(BF16)
        y2 = y2 + _dot(act, wdn_ref[c * FFN_CHUNK:(c + 1) * FFN_CHUNK, :])
    hout_ref[0] = hh_ref[hp - nh:hp, :]

    x2 = x1 + _rms(y2, gfpost_ref[...])
    e = _dot(p_ref[...].astype(BF16), wple_ref[...]) * _sigmoid(_dot(x2.astype(BF16), wpg_ref[...]))
    y_ref[...] = x2 + _rms(e, gple_ref[...])


def _post(x2, og, oa, sga, p2, hist, inter, seq_rows, wts):
    n = x2.shape[0]
    tm = min(ROW_TILE, seq_rows)
    tiles_per_seq = seq_rows // tm
    assert tiles_per_seq == 1 or inter == 1
    n_groups = n // seq_rows
    nh = (FFN_CONV_W - 1) * inter
    hp = _round_up(nh, SUBLANES)
    assert tm >= hp
    d_ple = p2.shape[1]
    row = lambda w: pl.BlockSpec((tm, w), lambda i: (i, 0))
    grp = pl.BlockSpec((1, nh, 2 * D_FF), lambda i: (i // tiles_per_seq, 0, 0))
    vec = _const_spec((1, D_MODEL))
    return pl.pallas_call(
        functools.partial(_post_kernel, inter=inter, tiles_per_seq=tiles_per_seq),
        grid=(n // tm,),
        in_specs=[row(D_MODEL), row(D_MODEL), row(D_MODEL), row(D_MODEL), row(d_ple), grp,
                  _const_spec((D_MODEL, D_MODEL)), vec, vec, _const_spec((D_MODEL, 2 * D_FF)),
                  _const_spec((FFN_CONV_W, 2 * D_FF)), _const_spec((1, 2 * D_FF)), _const_spec((D_FF, D_MODEL)), vec,
                  _const_spec((d_ple, D_MODEL)), _const_spec((D_MODEL, D_MODEL)), vec],
        out_specs=(row(D_MODEL), grp),
        out_shape=(jax.ShapeDtypeStruct((n, D_MODEL), F32),
                   jax.ShapeDtypeStruct((n_groups, nh, 2 * D_FF), F32)),
        scratch_shapes=[pltpu.VMEM((hp, 2 * D_FF), F32), pltpu.VMEM((D_FF // FFN_CHUNK, hp + tm, FFN_CHUNK), F32),
                        pltpu.VMEM((D_FF // FFN_CHUNK, hp + tm, FFN_CHUNK), F32)],
        compiler_params=pltpu.CompilerParams(dimension_semantics=("arbitrary",), vmem_limit_bytes=VMEM_LIMIT),
        name="post_ffn",
    )(x2, og, oa, sga, p2, hist, wts["w_out"], wts["g_mix_post"], wts["g_ffn_pre"], wts["w_up"], wts["ffn_cw"],
      wts["ffn_cb"], wts["w_down"], wts["g_ffn_post"], wts["w_ple"], wts["w_ple_gate"], wts["g_ple_post"])


def _prep_weights(l, norm_mix_pre, w_in, b_forget, lru_conv_w, lru_conv_b, lru_wa, lru_ba, lru_wx, lru_bx,
                  lru_lambda, w_out, norm_mix_post, norm_ffn_pre, w_up, ffn_conv_w, ffn_conv_b, w_down,
                  norm_ffn_post, w_ple, w_ple_gate, norm_ple_post):
    d = D_MODEL
    wi = w_in[l]
    offs = np.cumsum([0, d, d, d, d, N_HEADS, d, d])
    piece = lambda k: wi[:, int(offs[k]):int(offs[k + 1])].astype(BF16)
    vec = lambda a: a[l].reshape(1, -1).astype(F32)
    per_tile = MXU_DIM // LRU_BLOCK
    n_tiles = LRU_HEADS // per_tile

    def block_diag(w):
        w4 = w.reshape(n_tiles, per_tile, LRU_BLOCK, LRU_BLOCK)
        eye = jnp.eye(per_tile, dtype=w.dtype)
        return jnp.einsum("thij,hg->thigj", w4, eye).reshape(n_tiles, MXU_DIM, MXU_DIM)

    return {
        "g_mix_pre": vec(norm_mix_pre),
        "w_lru": piece(0), "w_q": piece(1), "w_k": piece(2), "w_v": piece(3),
        "w_f": jnp.pad(piece(4), ((0, 0), (0, LANES - N_HEADS))),
        "w_gl": piece(5), "w_ga": piece(6),
        "b_f": jnp.pad(vec(b_forget), ((0, 0), (0, LANES - N_HEADS))),
        "lru_cw": lru_conv_w[l].astype(F32), "lru_cb": vec(lru_conv_b),
        "lru_wab": jnp.concatenate([block_diag(lru_wa[l]), block_diag(lru_wx[l])], axis=-1).astype(BF16),
        "lru_ba": vec(lru_ba), "lru_bx": vec(lru_bx), "lru_lam": vec(lru_lambda),
        "w_out": w_out[l].astype(BF16), "g_mix_post": vec(norm_mix_post), "g_ffn_pre": vec(norm_ffn_pre),
        "w_up": w_up[l].astype(BF16), "ffn_cw": ffn_conv_w[l].astype(F32), "ffn_cb": vec(ffn_conv_b),
        "w_down": w_down[l].astype(BF16), "g_ffn_post": vec(norm_ffn_post),
        "w_ple": w_ple[l].astype(BF16), "w_ple_gate": w_ple_gate[l].astype(BF16), "g_ple_post": vec(norm_ple_post),
    }


def _prompt_layer(x, p_l, wts):
    b, s, d = x.shape
    n = b * s
    qa, ka, kf, vf, vt, lf, _, sga, og, h_last, lru_hist = _in_proj(
        x.reshape(n, d), jnp.zeros((b, LRU_CONV_W - 1, d), F32), jnp.zeros((b, 1, d), F32), 1, s, wts)
    oa = _attn_prompt(qa.reshape(b, s, -1), ka.reshape(b, s, -1), vt.reshape(b, N_HEADS, HEAD_DIM, s))
    y, ffn_hist = _post(x.reshape(n, d), og, oa.reshape(n, d), sga, p_l.reshape(n, -1),
                        jnp.zeros((b, FFN_CONV_W - 1, 2 * D_FF), F32), 1, s, wts)
    state = (kf.reshape(b, s, N_HEADS, HEAD_DIM), vf.reshape(b, s, N_HEADS, HEAD_DIM),
             lf[:, :N_HEADS].reshape(b, s, N_HEADS), h_last.reshape(b, d), lru_hist, ffn_hist)
    return y.reshape(b, s, d), state


def _sample_layer(x, p_l, cache_k, cache_v, cache_logf, page_ids, lru_buf, lru_h0, ffn_buf, wts):
    db, t, d = x.shape
    n = db * t
    tmaj = lambda a: jnp.swapaxes(a, 0, 1).reshape(a.shape[0] * a.shape[1], -1)
    bmaj = lambda a: jnp.swapaxes(a.reshape(-1, db, a.shape[-1]), 0, 1)

    xt = tmaj(x)
    qa, _, kf, vf, _, lf, c, sga, og, h_last, lru_hist = _in_proj(xt, tmaj(lru_buf)[None], lru_h0[None], db, n, wts)

    k_b, v_b, c_b = bmaj(kf), bmaj(vf), bmaj(c)[..., :N_HEADS]
    q_b = bmaj(qa).astype(F32).reshape(db, t, N_HEADS, AUG_DIM)[..., :HEAD_DIM].reshape(db, t, d)
    cq = jnp.swapaxes(c_b, 1, 2).reshape(db, N_HEADS * t, 1)
    ckn = jnp.pad(c_b.reshape(db, 1, t * N_HEADS), ((0, 0), (0, 0), (0, LANES - t * N_HEADS)))
    oa_b = _attn_sample(page_ids, cache_k, cache_v, cache_logf, q_b,
                        k_b.reshape(db, t * N_HEADS, HEAD_DIM), v_b.reshape(db, t * N_HEADS, HEAD_DIM), cq, ckn)

    y, ffn_hist = _post(xt, og, tmaj(oa_b), sga, tmaj(p_l), tmaj(ffn_buf)[None], db, n, wts)
    state = (k_b.reshape(db, t, N_HEADS, HEAD_DIM), v_b.reshape(db, t, N_HEADS, HEAD_DIM),
             bmaj(lf)[..., :N_HEADS], h_last.reshape(db, d),
             bmaj(lru_hist.reshape(-1, d)), bmaj(ffn_hist.reshape(-1, 2 * D_FF)))
    return bmaj(y), state


def kernel(x_prompt, x_sample, cache_k, cache_v, cache_logf, state_lru_h, state_lru_conv, state_ffn_conv, page_table, p_prompt, p_sample, norm_mix_pre, w_in, b_forget, lru_conv_w, lru_conv_b, lru_wa, lru_ba, lru_wx, lru_bx, lru_lambda, w_out, norm_mix_post, norm_ffn_pre, w_up, ffn_conv_w, ffn_conv_b, w_down, norm_ffn_post, w_ple, w_ple_gate, norm_ple_post):
    depth, n_pool, page = cache_k.shape[:3]
    ck = cache_k.reshape(depth * n_pool, page * N_HEADS, HEAD_DIM)
    cv = cache_v.reshape(depth * n_pool, page * N_HEADS, HEAD_DIM)
    cf = cache_logf.reshape(depth * n_pool, page * N_HEADS // LANES, LANES)
    y_p, y_s = x_prompt, x_sample
    outs_p, outs_s = [], []
    for l in range(depth):
        wts = _prep_weights(l, norm_mix_pre, w_in, b_forget, lru_conv_w, lru_conv_b, lru_wa, lru_ba, lru_wx, lru_bx,
                            lru_lambda, w_out, norm_mix_post, norm_ffn_pre, w_up, ffn_conv_w, ffn_conv_b, w_down,
                            norm_ffn_post, w_ple, w_ple_gate, norm_ple_post)
        y_p, st_p = _prompt_layer(y_p, p_prompt[l], wts)
        y_s, st_s = _sample_layer(y_s, p_sample[l], ck, cv, cf, page_table + l * n_pool,
                                  state_lru_conv[l], state_lru_h[l], state_ffn_conv[l], wts)
        outs_p.append(st_p)
        outs_s.append(st_s)
    stack = lambda outs, k: jnp.stack([o[k] for o in outs])
    res = [y_p, y_s]
    for k in range(6):
        res.append(stack(outs_p, k))
        res.append(stack(outs_s, k))
    return tuple(res)
```

```python
import functools

import numpy as np
import jax
import jax.numpy as jnp
from jax import lax
from jax.experimental import pallas as pl
from jax.experimental.pallas import tpu as pltpu

D_MODEL = 1024
N_HEADS = 8
HEAD_DIM = 128
ATT_SCALE = HEAD_DIM ** -0.5
LRU_HEADS = 16
LRU_BLOCK = 64
LRU_CONV_W = 4
LRU_C = 8.0
D_FF = 3 * D_MODEL
FFN_CONV_W = 3
RMS_EPS = 1e-6

LANES = 128
SUBLANES = 8
MXU_DIM = 256
AUG_DIM = 2 * HEAD_DIM
VMEM_LIMIT = 56 * 1024 * 1024

ROW_TILE = 512
ATT_TILE = 2048
ATT_KEY_CHUNK = MXU_DIM
FFN_CHUNK = 1024
PAGES_PER_STEP = 16
LOG2E = float(np.log2(np.e))

F32 = jnp.float32
BF16 = jnp.bfloat16


def _round_up(n, m):
    return (n + m - 1) // m * m


def _log2(n):
    assert n > 0 and n & (n - 1) == 0
    return n.bit_length() - 1


def _rms(x, g):
    ms = jnp.mean(x * x, axis=-1, keepdims=True)
    return (x * lax.rsqrt(ms + RMS_EPS)) * g


def _sigmoid(x):
    return 0.5 * jnp.tanh(0.5 * x) + 0.5


def _softplus(x):
    return jnp.maximum(x, 0.0) + jnp.log1p(jnp.exp(-jnp.abs(x)))


def _dot(a, b):
    return jnp.dot(a, b, preferred_element_type=F32)


def _dot_nt(a, b):
    return lax.dot_general(a, b, (((1,), (1,)), ((), ())), preferred_element_type=F32)


def _split3(c):
    hi = c.astype(BF16).astype(F32)
    r1 = c - hi
    mid = r1.astype(BF16).astype(F32)
    lo = (r1 - mid).astype(BF16).astype(F32)
    return hi, mid, lo


def _const_spec(shape):
    nd = len(shape)
    return pl.BlockSpec(shape, lambda *_: (0,) * nd, pipeline_mode=pl.Buffered(1))


def _in_proj_kernel(x_ref, g_ref, wl_ref, wq_ref, wk_ref, wv_ref, wf_ref, wgl_ref, wga_ref, bf_ref,
                    pq_ref, pk_ref, hist_ref, h0_ref, cw_ref, cb_ref, wab_ref, ba_ref, bx_ref, lam_ref,
                    qa_ref, ka_ref, kf_ref, vf_ref, vt_ref, lf_ref, c_ref, sga_ref, og_ref, hl_ref, hout_ref,
                    carry_ref, xp_ref, hcar_ref, *, inter, tiles_per_seq, steps):
    tm = x_ref.shape[0]
    i = pl.program_id(0)
    ub = _rms(x_ref[...], g_ref[...]).astype(BF16)

    lru = _lru_tile(ub, wl_ref, wgl_ref, i, hist_ref, h0_ref, cw_ref, cb_ref, wab_ref, ba_ref, bx_ref, lam_ref,
                    og_ref, hl_ref, hout_ref, xp_ref, hcar_ref, inter=inter, tiles_per_seq=tiles_per_seq, steps=steps)
    n_blk = D_MODEL // MXU_DIM
    kbs, qbs = [], []

    def piece(kind, blk):
        cols = slice(blk * MXU_DIM, (blk + 1) * MXU_DIM)
        if kind == "k":
            kblk = _dot(ub, wk_ref[:, cols])
            kf_ref[:, cols] = kblk
            kbs.append(kblk.astype(BF16))
        elif kind == "v":
            vblk = _dot(ub, wv_ref[:, cols])
            vf_ref[:, cols] = vblk
            vt_ref[0, cols, :] = vblk.T.astype(BF16)
        elif kind == "ga":
            sga_ref[:, cols] = _sigmoid(_dot(ub, wga_ref[:, cols])).astype(BF16)
        else:
            qbs.append((_dot(ub, wq_ref[:, cols]) * (ATT_SCALE * LOG2E)).astype(BF16))

    pieces = [(kind, blk) for kind in ("k", "v", "ga", "q") for blk in range(n_blk)]
    for stage in lru:
        if pieces:
            piece(*pieces.pop(0))
    while pieces:
        piece(*pieces.pop(0))
    qs = jnp.concatenate(qbs, axis=1)
    kb = jnp.concatenate(kbs, axis=1)

    lf = -_softplus(-(_dot(ub, wf_ref[...]) + bf_ref[...]))
    lf_ref[...] = lf
    row = lax.broadcasted_iota(jnp.int32, (tm, LANES), 0)
    c = lf
    d = inter
    while d < tm:
        c = c + jnp.where(row >= d, pltpu.roll(c, d, 0), 0.0)
        d *= 2
    if tiles_per_seq > 1:
        @pl.when(i % tiles_per_seq == 0)
        def _():
            carry_ref[...] = jnp.zeros_like(carry_ref)
        c = c + carry_ref[0:1, :]
        carry_ref[0:1, :] = c[tm - 1:tm, :]
    c_ref[...] = c

    hi, mid, lo = _split3(c * LOG2E)
    lane = lax.broadcasted_iota(jnp.int32, (tm, LANES), 1)
    cp = jnp.where(lane < N_HEADS, hi,
                   jnp.where(lane < 2 * N_HEADS, pltpu.roll(mid, N_HEADS, 1),
                             jnp.where(lane < 3 * N_HEADS, pltpu.roll(lo, 2 * N_HEADS, 1),
                                       jnp.where(lane == 3 * N_HEADS, 1.0, 0.0)))).astype(BF16)
    qx = _dot(cp, pq_ref[...]).astype(BF16)
    kx = _dot(cp, pk_ref[...]).astype(BF16)
    for h in range(N_HEADS):
        src = slice(h * HEAD_DIM, (h + 1) * HEAD_DIM)
        qa_ref[:, h * AUG_DIM:h * AUG_DIM + HEAD_DIM] = qs[:, src]
        qa_ref[:, h * AUG_DIM + HEAD_DIM:(h + 1) * AUG_DIM] = qx[:, src]
        ka_ref[:, h * AUG_DIM:h * AUG_DIM + HEAD_DIM] = kb[:, src]
        ka_ref[:, h * AUG_DIM + HEAD_DIM:(h + 1) * AUG_DIM] = kx[:, src]


def _bias_scatter_matrices():
    pq = np.zeros((LANES, D_MODEL), np.float32)
    pk = np.zeros((LANES, D_MODEL), np.float32)
    for h in range(N_HEADS):
        base = h * HEAD_DIM
        for part in range(3):
            pq[part * N_HEADS + h, base + part] = 1.0
            pq[3 * N_HEADS, base + 3 + part] = 1.0
            pk[3 * N_HEADS, base + part] = 1.0
            pk[part * N_HEADS + h, base + 3 + part] = -1.0
    return jnp.asarray(pq, BF16), jnp.asarray(pk, BF16)


def _in_proj(x2, hist, h0, inter, seq_rows, wts):
    n = x2.shape[0]
    tm = min(ROW_TILE, seq_rows)
    assert n % tm == 0 and seq_rows % tm == 0
    tiles_per_seq = seq_rows // tm
    assert tiles_per_seq == 1 or inter == 1
    n_groups = n // seq_rows
    nh = (LRU_CONV_W - 1) * inter
    assert tm >= nh
    hp = _round_up(nh, SUBLANES)
    pq, pk = _bias_scatter_matrices()
    row = lambda w: pl.BlockSpec((tm, w), lambda i: (i, 0))
    grp = lambda r: pl.BlockSpec((1, r, D_MODEL), lambda i: (i // tiles_per_seq, 0, 0))
    sq = _const_spec((D_MODEL, D_MODEL))
    vec = _const_spec((1, D_MODEL))
    out_shape = (
        jax.ShapeDtypeStruct((n, N_HEADS * AUG_DIM), BF16),
        jax.ShapeDtypeStruct((n, N_HEADS * AUG_DIM), BF16),
        jax.ShapeDtypeStruct((n, D_MODEL), F32),
        jax.ShapeDtypeStruct((n, D_MODEL), F32),
        jax.ShapeDtypeStruct((n_groups, D_MODEL, seq_rows), BF16),
        jax.ShapeDtypeStruct((n, LANES), F32),
        jax.ShapeDtypeStruct((n, LANES), F32),
        jax.ShapeDtypeStruct((n, D_MODEL), BF16),
        jax.ShapeDtypeStruct((n, D_MODEL), BF16),
        jax.ShapeDtypeStruct((n_groups, inter, D_MODEL), F32),
        jax.ShapeDtypeStruct((n_groups, nh, D_MODEL), F32),
    )
    vt_spec = pl.BlockSpec((1, D_MODEL, tm), lambda i: (i // tiles_per_seq, 0, i % tiles_per_seq))
    return pl.pallas_call(
        functools.partial(_in_proj_kernel, inter=inter, tiles_per_seq=tiles_per_seq, steps=seq_rows // inter),
        grid=(n // tm,),
        in_specs=[row(D_MODEL), vec, sq, sq, sq, sq,
                  _const_spec((D_MODEL, LANES)), sq, sq, _const_spec((1, LANES)),
                  _const_spec((LANES, D_MODEL)), _const_spec((LANES, D_MODEL)),
                  grp(nh), grp(inter), _const_spec((LRU_CONV_W, D_MODEL)), vec,
                  _const_spec((D_MODEL // MXU_DIM, MXU_DIM, 2 * MXU_DIM)), vec, vec, vec],
        out_specs=(row(N_HEADS * AUG_DIM), row(N_HEADS * AUG_DIM), row(D_MODEL), row(D_MODEL),
                   vt_spec, row(LANES), row(LANES), row(D_MODEL), row(D_MODEL), grp(inter), grp(nh)),
        out_shape=out_shape,
        scratch_shapes=[pltpu.VMEM((SUBLANES, LANES), F32),
                        pltpu.VMEM((hp + tm, D_MODEL), F32), pltpu.VMEM((max(inter, SUBLANES), D_MODEL), F32)],
        compiler_params=pltpu.CompilerParams(dimension_semantics=("arbitrary",), vmem_limit_bytes=VMEM_LIMIT),
        name="in_proj",
    )(x2, wts["g_mix_pre"], wts["w_lru"], wts["w_q"], wts["w_k"], wts["w_v"], wts["w_f"], wts["w_gl"],
      wts["w_ga"], wts["b_f"], pq, pk, hist, h0, wts["lru_cw"], wts["lru_cb"], wts["lru_wab"], wts["lru_ba"],
      wts["lru_bx"], wts["lru_lam"])


def _scan8(a, b):
    row = lax.broadcasted_iota(jnp.int32, a.shape, 0)
    for d in (1, 2, 4):
        keep = row >= d
        a_s = pltpu.roll(a, d, 0)
        b_s = pltpu.roll(b, d, 0)
        b = jnp.where(keep, a * b_s + b, b)
        a = jnp.where(keep, a * a_s, a)
    return a, b


def _lru_tile(ub, wl_ref, wgl_ref, i, hist_ref, h0_ref, cw_ref, cb_ref, wab_ref, ba_ref, bx_ref, lam_ref,
              og_ref, hl_ref, hout_ref, xp_ref, hcar_ref, *, inter, tiles_per_seq, steps):
    tt = ub.shape[0]
    nh = (LRU_CONV_W - 1) * inter
    hp = xp_ref.shape[0] - tt

    @pl.when(i % tiles_per_seq == 0)
    def _():
        xp_ref[hp - nh:hp, :] = hist_ref[0]
        hcar_ref[0:inter, :] = h0_ref[0]

    sp = _softplus(-lam_ref[...])
    for blk in range(D_MODEL // MXU_DIM):
        cols = slice(blk * MXU_DIM, (blk + 1) * MXU_DIM)
        xp_ref[hp:hp + tt, cols] = _dot(ub, wl_ref[:, cols])
        xc = cb_ref[:, cols]
        for j in range(LRU_CONV_W):
            off = hp - (LRU_CONV_W - 1 - j) * inter
            xc = xc + cw_ref[j:j + 1, cols] * xp_ref[off:off + tt, cols]
        xp_ref[hp - nh:hp, cols] = xp_ref[hp + tt - nh:hp + tt, cols]
        hout_ref[0, :, cols] = xp_ref[hp - nh:hp, cols]
        yield

        gates = _dot(xc.astype(BF16), wab_ref[blk])
        r = _sigmoid(gates[:, :MXU_DIM] + ba_ref[:, cols])
        ig = _sigmoid(gates[:, MXU_DIM:] + bx_ref[:, cols])
        log_a = (-LRU_C * r) * sp[:, cols]
        a = jnp.exp(log_a)
        th = jnp.tanh(log_a)
        b = jnp.sqrt((-2.0 * th) / (1.0 - th)) * (ig * xc)
        yield
        hs = []
        if inter == 1:
            carry = jnp.broadcast_to(hcar_ref[0:1, cols], (SUBLANES, MXU_DIM))
            for g in range(tt // SUBLANES):
                rows = slice(g * SUBLANES, (g + 1) * SUBLANES)
                a8, b8 = _scan8(a[rows, :], b[rows, :])
                h8 = a8 * carry + b8
                hs.append(h8)
                carry = jnp.broadcast_to(h8[SUBLANES - 1:SUBLANES, :], (SUBLANES, MXU_DIM))
            hcar_ref[0:1, cols] = carry[0:1, :]
        else:
            h = hcar_ref[0:inter, cols]
            for t in range(steps):
                rows = slice(t * inter, (t + 1) * inter)
                h = a[rows, :] * h + b[rows, :]
                hs.append(h)
            hcar_ref[0:inter, cols] = h
        sgl = _sigmoid(_dot(ub, wgl_ref[:, cols]))
        og_ref[:, cols] = (sgl * jnp.concatenate(hs, axis=0)).astype(BF16)
        yield
    hl_ref[0] = hcar_ref[0:inter, :]


def _attn_prompt_kernel(q_ref, k_ref, vt_ref, o_ref, sa_ref, sb_ref, xa_ref, xb_ref, m_ref, l_ref, acc_ref):
    tq = q_ref.shape[1]
    tk = sa_ref.shape[0]
    kc = min(ATT_KEY_CHUNK, tk)
    i = pl.program_id(2)

    def put_scores(j, dst, mx=None, lo=0):
        k0 = pl.multiple_of(j * tk, tk)
        sc = _dot_nt(k_ref[0, pl.ds(k0, tk), :], q_ref[0, lo:, :])
        dst[:, lo:] = sc
        if mx is not None:
            mx[...] = jnp.max(sc, axis=0, keepdims=True)

    def update(j, src, mx=None, lo=0, diagonal=False):
        k0 = pl.multiple_of(j * tk, tk)
        m = m_ref[:, lo:]
        if diagonal:
            kpos = lax.broadcasted_iota(jnp.int32, (tk, tk), 0)
            qpos = lax.broadcasted_iota(jnp.int32, (tk, tk), 1)
            src[:, lo:lo + tk] = jnp.where(kpos <= qpos, src[:, lo:lo + tk], -jnp.inf)
            m_new = m
            for c in range(tk // kc):
                m_new = jnp.maximum(m_new, jnp.max(src[c * kc:(c + 1) * kc, lo:], axis=0, keepdims=True))
        else:
            m_new = jnp.maximum(m, mx[...])
        alpha = jnp.exp2(m - m_new)
        m_ref[:, lo:] = m_new
        l = alpha * l_ref[:, lo:]
        acc = alpha * acc_ref[:, lo:]
        for c in range(tk // kc):
            p = jnp.exp2(src[c * kc:(c + 1) * kc, lo:] - m_new)
            l = l + jnp.sum(p, axis=0, keepdims=True)
            acc = acc + _dot(vt_ref[0, 0, :, pl.ds(k0 + c * kc, kc)], p.astype(BF16))
        l_ref[:, lo:] = l
        acc_ref[:, lo:] = acc

    m_ref[...] = jnp.full(m_ref.shape, -jnp.inf, F32)
    l_ref[...] = jnp.zeros(l_ref.shape, F32)
    acc_ref[...] = jnp.zeros(acc_ref.shape, F32)
    put_scores(0, sa_ref, xa_ref)

    def pair(jj, carry):
        j = 2 * jj
        put_scores(j + 1, sb_ref, xb_ref)
        update(j, sa_ref, xa_ref)
        put_scores(j + 2, sa_ref, xa_ref)
        update(j + 1, sb_ref, xb_ref)
        return carry

    lax.fori_loop(0, i, pair, 0)
    put_scores(2 * i + 1, sb_ref, lo=tk)
    update(2 * i, sa_ref, diagonal=True)
    update(2 * i + 1, sb_ref, lo=tk, diagonal=True)
    o_ref[0] = (acc_ref[...] / l_ref[...]).T.astype(o_ref.dtype)


def _attn_prompt(qa, ka, vt):
    b, s, _ = qa.shape
    tq = min(ATT_TILE, s)
    tk = tq // 2
    assert s % tq == 0 and tk % LANES == 0
    return pl.pallas_call(
        _attn_prompt_kernel,
        grid=(b, N_HEADS, s // tq),
        in_specs=[pl.BlockSpec((1, tq, AUG_DIM), lambda b_, h, i: (b_, i, h)),
                  pl.BlockSpec((1, s, AUG_DIM), lambda b_, h, i: (b_, 0, h)),
                  pl.BlockSpec((1, 1, HEAD_DIM, s), lambda b_, h, i: (b_, h, 0, 0))],
        out_specs=pl.BlockSpec((1, tq, HEAD_DIM), lambda b_, h, i: (b_, i, h)),
        out_shape=jax.ShapeDtypeStruct((b, s, D_MODEL), BF16),
        scratch_shapes=[pltpu.VMEM((tk, tq), F32), pltpu.VMEM((tk, tq), F32), pltpu.VMEM((1, tq), F32),
                        pltpu.VMEM((1, tq), F32), pltpu.VMEM((1, tq), F32), pltpu.VMEM((1, tq), F32),
                        pltpu.VMEM((HEAD_DIM, tq), F32)],
        compiler_params=pltpu.CompilerParams(dimension_semantics=("arbitrary", "arbitrary", "arbitrary"),
                                             vmem_limit_bytes=VMEM_LIMIT),
        name="attn_prompt",
    )(qa, ka, vt)


KEYS_PER_GROUP = LANES // N_HEADS


def _attn_sample_kernel(pt_ref, *refs, n_pb, n_grp):
    k_refs = refs[0:n_pb]
    v_refs = refs[n_pb:2 * n_pb]
    f_refs = refs[2 * n_pb:3 * n_pb]
    (q_ref, kn_ref, vn_ref, cq_ref, ckn_ref, uin_ref, usum_ref, o_ref,
     qall_ref, mask_ref, m_ref, l_ref, acc_ref, tail_ref) = refs[3 * n_pb:]
    del pt_ref
    t_new = q_ref.shape[1]
    nq = N_HEADS * t_new
    page_w = n_grp * LANES
    p_idx = pl.program_id(1)

    def online(s, values):
        m_old = m_ref[...]
        m_new = jnp.maximum(m_old, jnp.max(s, axis=-1, keepdims=True))
        alpha = jnp.exp2(m_old - m_new)
        p = jnp.exp2(s - m_new)
        m_ref[...] = m_new
        l_ref[...] = alpha * l_ref[...] + jnp.sum(p, axis=-1, keepdims=True)
        pb = p.astype(BF16)
        pv = None
        for (lo, hi), xv in values:
            term = _dot(pb[:, lo:hi], xv)
            pv = term if pv is None else pv + term
        acc_ref[...] = alpha * acc_ref[...] + pv

    @pl.when(p_idx == 0)
    def _():
        row = lax.broadcasted_iota(jnp.int32, (nq, LANES), 0)
        lane = lax.broadcasted_iota(jnp.int32, (nq, LANES), 1)
        q = q_ref[0]
        qall_ref[...] = jnp.concatenate([q[:, h * HEAD_DIM:(h + 1) * HEAD_DIM] for h in range(N_HEADS)],
                                        axis=0).astype(BF16)
        same_head = (lane & (N_HEADS - 1)) == (row >> _log2(t_new))
        mask_ref[...] = jnp.where(same_head, cq_ref[0] * LOG2E, -jnp.inf)
        m_ref[...] = jnp.full(m_ref.shape, -jnp.inf, F32)
        l_ref[...] = jnp.zeros(l_ref.shape, F32)
        acc_ref[...] = jnp.zeros(acc_ref.shape, F32)
        tail_ref[...] = jnp.zeros(tail_ref.shape, F32)
        pad = LANES - nq
        kn = jnp.concatenate([kn_ref[0], jnp.zeros((pad, HEAD_DIM), F32)], axis=0).astype(BF16)
        vn = jnp.concatenate([vn_ref[0], jnp.zeros((pad, HEAD_DIM), F32)], axis=0).astype(BF16)
        s = _dot_nt(qall_ref[...], kn) + (mask_ref[...] - ckn_ref[0] * LOG2E)
        s = jnp.where((lane >> _log2(N_HEADS)) <= (row & (t_new - 1)), s, -jnp.inf)
        online(s, [((0, LANES), vn)])

    qall = qall_ref[...]
    mask = mask_ref[...]
    tail = tail_ref[...]
    zeros8 = jnp.zeros((SUBLANES, LANES), F32)
    grp_row = lax.broadcasted_iota(jnp.int32, (n_grp, LANES), 0)
    s_parts = []
    for u in range(n_pb):
        lp = f_refs[u][0]
        hi, mid, lo = _split3(lp)
        l3 = jnp.concatenate([hi, mid, lo, zeros8], axis=0).astype(BF16)
        pin = _dot(l3, uin_ref[...])
        psum = _dot(l3, usum_ref[...])
        intra = pin[0:n_grp] + pin[n_grp:2 * n_grp] + pin[2 * n_grp:3 * n_grp]
        tot = psum[0:n_grp] + psum[n_grp:2 * n_grp] + psum[2 * n_grp:3 * n_grp]
        r = tot
        d = 1
        while d < n_grp:
            r = r + jnp.where(grp_row + d < n_grp, pltpu.roll(r, n_grp - d, 0), 0.0)
            d *= 2
        suf = (intra + (r - tot) + tail) * LOG2E
        tail = tail + r[0:1, :]
        sk = _dot_nt(qall, k_refs[u][0].astype(BF16))
        s_parts += [sk[:, g * LANES:(g + 1) * LANES] + (mask + suf[g:g + 1, :]) for g in range(n_grp)]
    tail_ref[...] = tail
    online(jnp.concatenate(s_parts, axis=1),
           [((u * page_w, (u + 1) * page_w), v_refs[u][0].astype(BF16)) for u in range(n_pb)])

    @pl.when(p_idx == pl.num_programs(1) - 1)
    def _():
        o = acc_ref[...] / l_ref[...]
        for h in range(N_HEADS):
            o_ref[0, :, h * HEAD_DIM:(h + 1) * HEAD_DIM] = o[h * t_new:(h + 1) * t_new, :]


def _suffix_matrices():
    idx = np.arange(LANES)
    key, head = idx // N_HEADS, idx % N_HEADS
    same = head[:, None] == head[None, :]
    later = same & (key[:, None] > key[None, :])
    return jnp.asarray(later, BF16), jnp.asarray(same, BF16)


def _attn_sample(page_ids, cache_k, cache_v, cache_logf, q, kn, vn, cq, ckn):
    db, n_pages = page_ids.shape
    page_rows = cache_k.shape[1]
    n_grp = page_rows // LANES
    t_new = q.shape[1]
    nq = N_HEADS * t_new
    assert t_new == SUBLANES and n_grp == SUBLANES and cache_logf.shape[1:] == (n_grp, LANES)
    n_pb = PAGES_PER_STEP if n_pages % PAGES_PER_STEP == 0 else 1
    later, same = _suffix_matrices()

    def page_map(u):
        def index_map(b, p, pt):
            return (pt[b * n_pages + (n_pages - 1 - (p * n_pb + u))], 0, 0)
        return index_map

    per_b = lambda r, w: pl.BlockSpec((1, r, w), lambda b, p, pt: (b, 0, 0))
    const = pl.BlockSpec((LANES, LANES), lambda b, p, pt: (0, 0))
    in_specs = ([pl.BlockSpec((1, page_rows, HEAD_DIM), page_map(u)) for u in range(n_pb)]
                + [pl.BlockSpec((1, page_rows, HEAD_DIM), page_map(u)) for u in range(n_pb)]
                + [pl.BlockSpec((1, n_grp, LANES), page_map(u)) for u in range(n_pb)]
                + [per_b(t_new, D_MODEL), per_b(nq, HEAD_DIM), per_b(nq, HEAD_DIM), per_b(nq, 1), per_b(1, LANES),
                   const, const])
    grid_spec = pltpu.PrefetchScalarGridSpec(
        num_scalar_prefetch=1,
        grid=(db, n_pages // n_pb),
        in_specs=in_specs,
        out_specs=pl.BlockSpec((1, t_new, D_MODEL), lambda b, p, pt: (b, 0, 0)),
        scratch_shapes=[pltpu.VMEM((nq, HEAD_DIM), BF16), pltpu.VMEM((nq, LANES), F32), pltpu.VMEM((nq, 1), F32),
                        pltpu.VMEM((nq, 1), F32), pltpu.VMEM((nq, HEAD_DIM), F32), pltpu.VMEM((1, LANES), F32)],
    )
    return pl.pallas_call(
        functools.partial(_attn_sample_kernel, n_pb=n_pb, n_grp=n_grp),
        grid_spec=grid_spec,
        out_shape=jax.ShapeDtypeStruct((db, t_new, D_MODEL), F32),
        compiler_params=pltpu.CompilerParams(dimension_semantics=("arbitrary", "arbitrary"),
                                             vmem_limit_bytes=VMEM_LIMIT),
        name="attn_sample",
    )(page_ids.reshape(-1), *([cache_k] * n_pb), *([cache_v] * n_pb), *([cache_logf] * n_pb),
      q, kn, vn, cq, ckn, later, same)


def _gelu_tanh(x):
    return 0.5 * x * (1.0 + jnp.tanh(np.float32(np.sqrt(2.0 / np.pi)) * (x + 0.044715 * (x * x * x))))


def _post_kernel(x_ref, og_ref, oa_ref, sga_ref, p_ref, hist_ref,
                 wo_ref, gpost_ref, gpre_ref, wup_ref, fcw_ref, fcb_ref, wdn_ref, gfpost_ref,
                 wple_ref, wpg_ref, gple_ref,
                 y_ref, hout_ref,
                 hh_ref, xg_ref, xv_ref, *, inter, tiles_per_seq):
    tm = x_ref.shape[0]
    nh = (FFN_CONV_W - 1) * inter
    hp = hh_ref.shape[0]
    i = pl.program_id(0)

    @pl.when(i % tiles_per_seq == 0)
    def _():
        if hp > nh:
            hh_ref[...] = jnp.zeros_like(hh_ref)
        hh_ref[hp - nh:hp, :] = hist_ref[0]

    merged = og_ref[...].astype(F32) + sga_ref[...].astype(F32) * oa_ref[...].astype(F32)
    x1 = x_ref[...] + _rms(_dot(merged.astype(BF16), wo_ref[...]), gpost_ref[...])
    hn = _rms(x1, gpre_ref[...]).astype(BF16)

    n_chunks = D_FF // FFN_CHUNK

    def up_conv(c):
        halves = []
        for xp_ref, base in ((xg_ref, 0), (xv_ref, D_FF)):
            cols = slice(base + c * FFN_CHUNK, base + (c + 1) * FFN_CHUNK)
            xp_ref[c % 2, 0:hp, :] = hh_ref[:, cols]
            xp_ref[c % 2, hp:hp + tm, :] = _dot(hn, wup_ref[:, cols])
            hc = fcb_ref[:, cols]
            for j in range(FFN_CONV_W):
                off = hp - (FFN_CONV_W - 1 - j) * inter
                hc = hc + fcw_ref[j:j + 1, cols] * xp_ref[c % 2, off:off + tm, :]
            hh_ref[:, cols] = xp_ref[c % 2, tm:tm + hp, :]
            halves.append(hc)
        return halves

    y2 = jnp.zeros((tm, D_MODEL), F32)
    nxt = up_conv(0)
    for c in range(n_chunks):
        cur = nxt
        if c + 1 < n_chunks:
            nxt = up_conv(c + 1)
        act = (_gelu_tanh(cur[0]) * cur[1]).astype(BF16)
        y2 = y2 + _dot(act, wdn_ref[c * FFN_CHUNK:(c + 1) * FFN_CHUNK, :])
    hout_ref[0] = hh_ref[hp - nh:hp, :]

    x2 = x1 + _rms(y2, gfpost_ref[...])
    e = _dot(p_ref[...].astype(BF16), wple_ref[...]) * _sigmoid(_dot(x2.astype(BF16), wpg_ref[...]))
    y_ref[...] = x2 + _rms(e, gple_ref[...])


def _post(x2, og, oa, sga, p2, hist, inter, seq_rows, wts):
    n = x2.shape[0]
    tm = min(ROW_TILE, seq_rows)
    tiles_per_seq = seq_rows // tm
    assert tiles_per_seq == 1 or inter == 1
    n_groups = n // seq_rows
    nh = (FFN_CONV_W - 1) * inter
    hp = _round_up(nh, SUBLANES)
    assert tm >= hp
    d_ple = p2.shape[1]
    row = lambda w: pl.BlockSpec((tm, w), lambda i: (i, 0))
    grp = pl.BlockSpec((1, nh, 2 * D_FF), lambda i: (i // tiles_per_seq, 0, 0))
    vec = _const_spec((1, D_MODEL))
    return pl.pallas_call(
        functools.partial(_post_kernel, inter=inter, tiles_per_seq=tiles_per_seq),
        grid=(n // tm,),
        in_specs=[row(D_MODEL), row(D_MODEL), row(D_MODEL), row(D_MODEL), row(d_ple), grp,
                  _const_spec((D_MODEL, D_MODEL)), vec, vec, _const_spec((D_MODEL, 2 * D_FF)),
                  _const_spec((FFN_CONV_W, 2 * D_FF)), _const_spec((1, 2 * D_FF)), _const_spec((D_FF, D_MODEL)), vec,
                  _const_spec((d_ple, D_MODEL)), _const_spec((D_MODEL, D_MODEL)), vec],
        out_specs=(row(D_MODEL), grp),
        out_shape=(jax.ShapeDtypeStruct((n, D_MODEL), F32),
                   jax.ShapeDtypeStruct((n_groups, nh, 2 * D_FF), F32)),
        scratch_shapes=[pltpu.VMEM((hp, 2 * D_FF), F32), pltpu.VMEM((2, hp + tm, FFN_CHUNK), F32),
                        pltpu.VMEM((2, hp + tm, FFN_CHUNK), F32)],
        compiler_params=pltpu.CompilerParams(dimension_semantics=("arbitrary",), vmem_limit_bytes=VMEM_LIMIT),
        name="post_ffn",
    )(x2, og, oa, sga, p2, hist, wts["w_out"], wts["g_mix_post"], wts["g_ffn_pre"], wts["w_up"], wts["ffn_cw"],
      wts["ffn_cb"], wts["w_down"], wts["g_ffn_post"], wts["w_ple"], wts["w_ple_gate"], wts["g_ple_post"])


def _prep_weights(l, norm_mix_pre, w_in, b_forget, lru_conv_w, lru_conv_b, lru_wa, lru_ba, lru_wx, lru_bx,
                  lru_lambda, w_out, norm_mix_post, norm_ffn_pre, w_up, ffn_conv_w, ffn_conv_b, w_down,
                  norm_ffn_post, w_ple, w_ple_gate, norm_ple_post):
    d = D_MODEL
    wi = w_in[l]
    offs = np.cumsum([0, d, d, d, d, N_HEADS, d, d])
    piece = lambda k: wi[:, int(offs[k]):int(offs[k + 1])].astype(BF16)
    vec = lambda a: a[l].reshape(1, -1).astype(F32)
    per_tile = MXU_DIM // LRU_BLOCK
    n_tiles = LRU_HEADS // per_tile

    def block_diag(w):
        w4 = w.reshape(n_tiles, per_tile, LRU_BLOCK, LRU_BLOCK)
        eye = jnp.eye(per_tile, dtype=w.dtype)
        return jnp.einsum("thij,hg->thigj", w4, eye).reshape(n_tiles, MXU_DIM, MXU_DIM)

    return {
        "g_mix_pre": vec(norm_mix_pre),
        "w_lru": piece(0), "w_q": piece(1), "w_k": piece(2), "w_v": piece(3),
        "w_f": jnp.pad(piece(4), ((0, 0), (0, LANES - N_HEADS))),
        "w_gl": piece(5), "w_ga": piece(6),
        "b_f": jnp.pad(vec(b_forget), ((0, 0), (0, LANES - N_HEADS))),
        "lru_cw": lru_conv_w[l].astype(F32), "lru_cb": vec(lru_conv_b),
        "lru_wab": jnp.concatenate([block_diag(lru_wa[l]), block_diag(lru_wx[l])], axis=-1).astype(BF16),
        "lru_ba": vec(lru_ba), "lru_bx": vec(lru_bx), "lru_lam": vec(lru_lambda),
        "w_out": w_out[l].astype(BF16), "g_mix_post": vec(norm_mix_post), "g_ffn_pre": vec(norm_ffn_pre),
        "w_up": w_up[l].astype(BF16), "ffn_cw": ffn_conv_w[l].astype(F32), "ffn_cb": vec(ffn_conv_b),
        "w_down": w_down[l].astype(BF16), "g_ffn_post": vec(norm_ffn_post),
        "w_ple": w_ple[l].astype(BF16), "w_ple_gate": w_ple_gate[l].astype(BF16), "g_ple_post": vec(norm_ple_post),
    }


def _prompt_layer(x, p_l, wts):
    b, s, d = x.shape
    n = b * s
    qa, ka, kf, vf, vt, lf, _, sga, og, h_last, lru_hist = _in_proj(
        x.reshape(n, d), jnp.zeros((b, LRU_CONV_W - 1, d), F32), jnp.zeros((b, 1, d), F32), 1, s, wts)
    oa = _attn_prompt(qa.reshape(b, s, -1), ka.reshape(b, s, -1), vt.reshape(b, N_HEADS, HEAD_DIM, s))
    y, ffn_hist = _post(x.reshape(n, d), og, oa.reshape(n, d), sga, p_l.reshape(n, -1),
                        jnp.zeros((b, FFN_CONV_W - 1, 2 * D_FF), F32), 1, s, wts)
    state = (kf.reshape(b, s, N_HEADS, HEAD_DIM), vf.reshape(b, s, N_HEADS, HEAD_DIM),
             lf[:, :N_HEADS].reshape(b, s, N_HEADS), h_last.reshape(b, d), lru_hist, ffn_hist)
    return y.reshape(b, s, d), state


def _sample_layer(x, p_l, cache_k, cache_v, cache_logf, page_ids, lru_buf, lru_h0, ffn_buf, wts):
    db, t, d = x.shape
    n = db * t
    tmaj = lambda a: jnp.swapaxes(a, 0, 1).reshape(a.shape[0] * a.shape[1], -1)
    bmaj = lambda a: jnp.swapaxes(a.reshape(-1, db, a.shape[-1]), 0, 1)

    xt = tmaj(x)
    qa, _, kf, vf, _, lf, c, sga, og, h_last, lru_hist = _in_proj(xt, tmaj(lru_buf)[None], lru_h0[None], db, n, wts)

    k_b, v_b, c_b = bmaj(kf), bmaj(vf), bmaj(c)[..., :N_HEADS]
    q_b = bmaj(qa).astype(F32).reshape(db, t, N_HEADS, AUG_DIM)[..., :HEAD_DIM].reshape(db, t, d)
    cq = jnp.swapaxes(c_b, 1, 2).reshape(db, N_HEADS * t, 1)
    ckn = jnp.pad(c_b.reshape(db, 1, t * N_HEADS), ((0, 0), (0, 0), (0, LANES - t * N_HEADS)))
    oa_b = _attn_sample(page_ids, cache_k, cache_v, cache_logf, q_b,
                        k_b.reshape(db, t * N_HEADS, HEAD_DIM), v_b.reshape(db, t * N_HEADS, HEAD_DIM), cq, ckn)

    y, ffn_hist = _post(xt, og, tmaj(oa_b), sga, tmaj(p_l), tmaj(ffn_buf)[None], db, n, wts)
    state = (k_b.reshape(db, t, N_HEADS, HEAD_DIM), v_b.reshape(db, t, N_HEADS, HEAD_DIM),
             bmaj(lf)[..., :N_HEADS], h_last.reshape(db, d),
             bmaj(lru_hist.reshape(-1, d)), bmaj(ffn_hist.reshape(-1, 2 * D_FF)))
    return bmaj(y), state


def kernel(x_prompt, x_sample, cache_k, cache_v, cache_logf, state_lru_h, state_lru_conv, state_ffn_conv, page_table, p_prompt, p_sample, norm_mix_pre, w_in, b_forget, lru_conv_w, lru_conv_b, lru_wa, lru_ba, lru_wx, lru_bx, lru_lambda, w_out, norm_mix_post, norm_ffn_pre, w_up, ffn_conv_w, ffn_conv_b, w_down, norm_ffn_post, w_ple, w_ple_gate, norm_ple_post):
    depth, n_pool, page = cache_k.shape[:3]
    ck = cache_k.reshape(depth * n_pool, page * N_HEADS, HEAD_DIM)
    cv = cache_v.reshape(depth * n_pool, page * N_HEADS, HEAD_DIM)
    cf = cache_logf.reshape(depth * n_pool, page * N_HEADS // LANES, LANES)
    y_p, y_s = x_prompt, x_sample
    outs_p, outs_s = [], []
    for l in range(depth):
        wts = _prep_weights(l, norm_mix_pre, w_in, b_forget, lru_conv_w, lru_conv_b, lru_wa, lru_ba, lru_wx, lru_bx,
                            lru_lambda, w_out, norm_mix_post, norm_ffn_pre, w_up, ffn_conv_w, ffn_conv_b, w_down,
                            norm_ffn_post, w_ple, w_ple_gate, norm_ple_post)
        y_p, st_p = _prompt_layer(y_p, p_prompt[l], wts)
        y_s, st_s = _sample_layer(y_s, p_sample[l], ck, cv, cf, page_table + l * n_pool,
                                  state_lru_conv[l], state_lru_h[l], state_ffn_conv[l], wts)
        outs_p.append(st_p)
        outs_s.append(st_s)
    stack = lambda outs, k: jnp.stack([o[k] for o in outs])
    res = [y_p, y_s]
    for k in range(6):
        res.append(stack(outs_p, k))
        res.append(stack(outs_s, k))
    return tuple(res)
```

```python
import functools

import numpy as np
import jax
import jax.numpy as jnp
from jax import lax
from jax.experimental import pallas as pl
from jax.experimental.pallas import tpu as pltpu

D_MODEL = 1024
N_HEADS = 8
HEAD_DIM = 128
ATT_SCALE = HEAD_DIM ** -0.5
LRU_HEADS = 16
LRU_BLOCK = 64
LRU_CONV_W = 4
LRU_C = 8.0
D_FF = 3 * D_MODEL
FFN_CONV_W = 3
RMS_EPS = 1e-6

LANES = 128
SUBLANES = 8
MXU_DIM = 256
AUG_DIM = 2 * HEAD_DIM
VMEM_LIMIT = 56 * 1024 * 1024

ROW_TILE = 512
ATT_TILE = 2048
ATT_KEY_CHUNK = MXU_DIM
FFN_CHUNK = 1024
PAGES_PER_STEP = 16
LOG2E = float(np.log2(np.e))

F32 = jnp.float32
BF16 = jnp.bfloat16


def _round_up(n, m):
    return (n + m - 1) // m * m


def _log2(n):
    assert n > 0 and n & (n - 1) == 0
    return n.bit_length() - 1


def _rms(x, g):
    ms = jnp.mean(x * x, axis=-1, keepdims=True)
    return (x * lax.rsqrt(ms + RMS_EPS)) * g


def _sigmoid(x):
    return 0.5 * jnp.tanh(0.5 * x) + 0.5


def _softplus(x):
    return jnp.maximum(x, 0.0) + jnp.log1p(jnp.exp(-jnp.abs(x)))


def _dot(a, b):
    return jnp.dot(a, b, preferred_element_type=F32)


def _dot_nt(a, b):
    return lax.dot_general(a, b, (((1,), (1,)), ((), ())), preferred_element_type=F32)


def _split3(c):
    hi = c.astype(BF16).astype(F32)
    r1 = c - hi
    mid = r1.astype(BF16).astype(F32)
    lo = (r1 - mid).astype(BF16).astype(F32)
    return hi, mid, lo


def _const_spec(shape):
    nd = len(shape)
    return pl.BlockSpec(shape, lambda *_: (0,) * nd, pipeline_mode=pl.Buffered(1))


def _in_proj_kernel(x_ref, g_ref, wl_ref, wq_ref, wk_ref, wv_ref, wf_ref, wgl_ref, wga_ref, bf_ref,
                    pq_ref, pk_ref, hist_ref, h0_ref, cw_ref, cb_ref, wab_ref, ba_ref, bx_ref, lam_ref,
                    qa_ref, ka_ref, kf_ref, vf_ref, vt_ref, lf_ref, c_ref, sga_ref, og_ref, hl_ref, hout_ref,
                    carry_ref, xp_ref, hcar_ref, *, inter, tiles_per_seq, steps):
    tm = x_ref.shape[0]
    i = pl.program_id(0)
    ub = _rms(x_ref[...], g_ref[...]).astype(BF16)

    lru = _lru_tile(ub, wl_ref, wgl_ref, i, hist_ref, h0_ref, cw_ref, cb_ref, wab_ref, ba_ref, bx_ref, lam_ref,
                    og_ref, hl_ref, hout_ref, xp_ref, hcar_ref, inter=inter, tiles_per_seq=tiles_per_seq, steps=steps)
    n_blk = D_MODEL // MXU_DIM
    kbs, qbs = [], []

    def store_heads(dst_ref, blk, val):
        for hh in range(MXU_DIM // HEAD_DIM):
            head = blk * (MXU_DIM // HEAD_DIM) + hh
            dst_ref[pl.ds(head, tm, stride=N_HEADS), :] = val[:, hh * HEAD_DIM:(hh + 1) * HEAD_DIM]

    def piece(kind, blk):
        cols = slice(blk * MXU_DIM, (blk + 1) * MXU_DIM)
        if kind == "k":
            kblk = _dot(ub, wk_ref[:, cols])
            store_heads(kf_ref, blk, kblk)
            kbs.append(kblk.astype(BF16))
        elif kind == "v":
            vblk = _dot(ub, wv_ref[:, cols])
            store_heads(vf_ref, blk, vblk)
            vt_ref[0, cols, :] = vblk.T.astype(BF16)
        elif kind == "ga":
            sga_ref[:, cols] = _sigmoid(_dot(ub, wga_ref[:, cols])).astype(BF16)
        else:
            qbs.append((_dot(ub, wq_ref[:, cols]) * (ATT_SCALE * LOG2E)).astype(BF16))

    pieces = [(kind, blk) for kind in ("k", "v", "ga", "q") for blk in range(n_blk)]
    for stage in lru:
        if pieces:
            piece(*pieces.pop(0))
    while pieces:
        piece(*pieces.pop(0))
    qs = jnp.concatenate(qbs, axis=1)
    kb = jnp.concatenate(kbs, axis=1)

    lf = -_softplus(-(_dot(ub, wf_ref[...]) + bf_ref[...]))
    lf_ref[...] = lf
    row = lax.broadcasted_iota(jnp.int32, (tm, LANES), 0)
    c = lf
    d = inter
    while d < tm:
        c = c + jnp.where(row >= d, pltpu.roll(c, d, 0), 0.0)
        d *= 2
    if tiles_per_seq > 1:
        @pl.when(i % tiles_per_seq == 0)
        def _():
            carry_ref[...] = jnp.zeros_like(carry_ref)
        c = c + carry_ref[0:1, :]
        carry_ref[0:1, :] = c[tm - 1:tm, :]
    c_ref[...] = c

    hi, mid, lo = _split3(c * LOG2E)
    lane = lax.broadcasted_iota(jnp.int32, (tm, LANES), 1)
    cp = jnp.where(lane < N_HEADS, hi,
                   jnp.where(lane < 2 * N_HEADS, pltpu.roll(mid, N_HEADS, 1),
                             jnp.where(lane < 3 * N_HEADS, pltpu.roll(lo, 2 * N_HEADS, 1),
                                       jnp.where(lane == 3 * N_HEADS, 1.0, 0.0)))).astype(BF16)
    qx = _dot(cp, pq_ref[...]).astype(BF16)
    kx = _dot(cp, pk_ref[...]).astype(BF16)
    for h in range(N_HEADS):
        src = slice(h * HEAD_DIM, (h + 1) * HEAD_DIM)
        qa_ref[:, h * AUG_DIM:h * AUG_DIM + HEAD_DIM] = qs[:, src]
        qa_ref[:, h * AUG_DIM + HEAD_DIM:(h + 1) * AUG_DIM] = qx[:, src]
        ka_ref[:, h * AUG_DIM:h * AUG_DIM + HEAD_DIM] = kb[:, src]
        ka_ref[:, h * AUG_DIM + HEAD_DIM:(h + 1) * AUG_DIM] = kx[:, src]


def _bias_scatter_matrices():
    pq = np.zeros((LANES, D_MODEL), np.float32)
    pk = np.zeros((LANES, D_MODEL), np.float32)
    for h in range(N_HEADS):
        base = h * HEAD_DIM
        for part in range(3):
            pq[part * N_HEADS + h, base + part] = 1.0
            pq[3 * N_HEADS, base + 3 + part] = 1.0
            pk[3 * N_HEADS, base + part] = 1.0
            pk[part * N_HEADS + h, base + 3 + part] = -1.0
    return jnp.asarray(pq, BF16), jnp.asarray(pk, BF16)


def _in_proj(x2, hist, h0, inter, seq_rows, wts):
    n = x2.shape[0]
    tm = min(ROW_TILE, seq_rows)
    assert n % tm == 0 and seq_rows % tm == 0
    tiles_per_seq = seq_rows // tm
    assert tiles_per_seq == 1 or inter == 1
    n_groups = n // seq_rows
    nh = (LRU_CONV_W - 1) * inter
    assert tm >= nh
    hp = _round_up(nh, SUBLANES)
    pq, pk = _bias_scatter_matrices()
    row = lambda w: pl.BlockSpec((tm, w), lambda i: (i, 0))
    grp = lambda r: pl.BlockSpec((1, r, D_MODEL), lambda i: (i // tiles_per_seq, 0, 0))
    sq = _const_spec((D_MODEL, D_MODEL))
    vec = _const_spec((1, D_MODEL))
    out_shape = (
        jax.ShapeDtypeStruct((n, N_HEADS * AUG_DIM), BF16),
        jax.ShapeDtypeStruct((n, N_HEADS * AUG_DIM), BF16),
        jax.ShapeDtypeStruct((n * N_HEADS, HEAD_DIM), F32),
        jax.ShapeDtypeStruct((n * N_HEADS, HEAD_DIM), F32),
        jax.ShapeDtypeStruct((n_groups, D_MODEL, seq_rows), BF16),
        jax.ShapeDtypeStruct((n, LANES), F32),
        jax.ShapeDtypeStruct((n, LANES), F32),
        jax.ShapeDtypeStruct((n, D_MODEL), BF16),
        jax.ShapeDtypeStruct((n, D_MODEL), BF16),
        jax.ShapeDtypeStruct((n_groups, inter, D_MODEL), F32),
        jax.ShapeDtypeStruct((n_groups, nh, D_MODEL), F32),
    )
    vt_spec = pl.BlockSpec((1, D_MODEL, tm), lambda i: (i // tiles_per_seq, 0, i % tiles_per_seq))
    head_rows = pl.BlockSpec((tm * N_HEADS, HEAD_DIM), lambda i: (i, 0))
    return pl.pallas_call(
        functools.partial(_in_proj_kernel, inter=inter, tiles_per_seq=tiles_per_seq, steps=seq_rows // inter),
        grid=(n // tm,),
        in_specs=[row(D_MODEL), vec, sq, sq, sq, sq,
                  _const_spec((D_MODEL, LANES)), sq, sq, _const_spec((1, LANES)),
                  _const_spec((LANES, D_MODEL)), _const_spec((LANES, D_MODEL)),
                  grp(nh), grp(inter), _const_spec((LRU_CONV_W, D_MODEL)), vec,
                  _const_spec((D_MODEL // MXU_DIM, MXU_DIM, 2 * MXU_DIM)), vec, vec, vec],
        out_specs=(row(N_HEADS * AUG_DIM), row(N_HEADS * AUG_DIM), head_rows, head_rows,
                   vt_spec, row(LANES), row(LANES), row(D_MODEL), row(D_MODEL), grp(inter), grp(nh)),
        out_shape=out_shape,
        scratch_shapes=[pltpu.VMEM((SUBLANES, LANES), F32),
                        pltpu.VMEM((hp + tm, D_MODEL), F32), pltpu.VMEM((max(inter, SUBLANES), D_MODEL), F32)],
        compiler_params=pltpu.CompilerParams(dimension_semantics=("arbitrary",), vmem_limit_bytes=VMEM_LIMIT),
        name="in_proj",
    )(x2, wts["g_mix_pre"], wts["w_lru"], wts["w_q"], wts["w_k"], wts["w_v"], wts["w_f"], wts["w_gl"],
      wts["w_ga"], wts["b_f"], pq, pk, hist, h0, wts["lru_cw"], wts["lru_cb"], wts["lru_wab"], wts["lru_ba"],
      wts["lru_bx"], wts["lru_lam"])


def _scan8(a, b):
    row = lax.broadcasted_iota(jnp.int32, a.shape, 0)
    for d in (1, 2, 4):
        keep = row >= d
        a_s = pltpu.roll(a, d, 0)
        b_s = pltpu.roll(b, d, 0)
        b = jnp.where(keep, a * b_s + b, b)
        a = jnp.where(keep, a * a_s, a)
    return a, b


def _lru_tile(ub, wl_ref, wgl_ref, i, hist_ref, h0_ref, cw_ref, cb_ref, wab_ref, ba_ref, bx_ref, lam_ref,
              og_ref, hl_ref, hout_ref, xp_ref, hcar_ref, *, inter, tiles_per_seq, steps):
    tt = ub.shape[0]
    nh = (LRU_CONV_W - 1) * inter
    hp = xp_ref.shape[0] - tt

    @pl.when(i % tiles_per_seq == 0)
    def _():
        xp_ref[hp - nh:hp, :] = hist_ref[0]
        hcar_ref[0:inter, :] = h0_ref[0]

    sp = _softplus(-lam_ref[...])
    for blk in range(D_MODEL // MXU_DIM):
        cols = slice(blk * MXU_DIM, (blk + 1) * MXU_DIM)
        xp_ref[hp:hp + tt, cols] = _dot(ub, wl_ref[:, cols])
        xc = cb_ref[:, cols]
        for j in range(LRU_CONV_W):
            off = hp - (LRU_CONV_W - 1 - j) * inter
            xc = xc + cw_ref[j:j + 1, cols] * xp_ref[off:off + tt, cols]
        xp_ref[hp - nh:hp, cols] = xp_ref[hp + tt - nh:hp + tt, cols]
        hout_ref[0, :, cols] = xp_ref[hp - nh:hp, cols]
        yield

        gates = _dot(xc.astype(BF16), wab_ref[blk])
        r = _sigmoid(gates[:, :MXU_DIM] + ba_ref[:, cols])
        ig = _sigmoid(gates[:, MXU_DIM:] + bx_ref[:, cols])
        log_a = (-LRU_C * r) * sp[:, cols]
        a = jnp.exp(log_a)
        th = jnp.tanh(log_a)
        b = jnp.sqrt((-2.0 * th) / (1.0 - th)) * (ig * xc)
        yield
        hs = []
        if inter == 1:
            carry = jnp.broadcast_to(hcar_ref[0:1, cols], (SUBLANES, MXU_DIM))
            for g in range(tt // SUBLANES):
                rows = slice(g * SUBLANES, (g + 1) * SUBLANES)
                a8, b8 = _scan8(a[rows, :], b[rows, :])
                h8 = a8 * carry + b8
                hs.append(h8)
                carry = jnp.broadcast_to(h8[SUBLANES - 1:SUBLANES, :], (SUBLANES, MXU_DIM))
            hcar_ref[0:1, cols] = carry[0:1, :]
        else:
            h = hcar_ref[0:inter, cols]
            for t in range(steps):
                rows = slice(t * inter, (t + 1) * inter)
                h = a[rows, :] * h + b[rows, :]
                hs.append(h)
            hcar_ref[0:inter, cols] = h
        sgl = _sigmoid(_dot(ub, wgl_ref[:, cols]))
        og_ref[:, cols] = (sgl * jnp.concatenate(hs, axis=0)).astype(BF16)
        yield
    hl_ref[0] = hcar_ref[0:inter, :]


def _attn_prompt_kernel(q_ref, k_ref, vt_ref, o_ref, sa_ref, sb_ref, xa_ref, xb_ref, m_ref, l_ref, acc_ref):
    tq = q_ref.shape[1]
    tk = sa_ref.shape[0]
    kc = min(ATT_KEY_CHUNK, tk)
    i = pl.program_id(2)

    def put_scores(j, dst, mx=None, lo=0):
        k0 = pl.multiple_of(j * tk, tk)
        sc = _dot_nt(k_ref[0, pl.ds(k0, tk), :], q_ref[0, lo:, :])
        dst[:, lo:] = sc
        if mx is not None:
            mx[...] = jnp.max(sc, axis=0, keepdims=True)

    def update(j, src, mx=None, lo=0, diagonal=False):
        k0 = pl.multiple_of(j * tk, tk)
        m = m_ref[:, lo:]
        if diagonal:
            kpos = lax.broadcasted_iota(jnp.int32, (tk, tk), 0)
            qpos = lax.broadcasted_iota(jnp.int32, (tk, tk), 1)
            src[:, lo:lo + tk] = jnp.where(kpos <= qpos, src[:, lo:lo + tk], -jnp.inf)
            m_new = m
            for c in range(tk // kc):
                m_new = jnp.maximum(m_new, jnp.max(src[c * kc:(c + 1) * kc, lo:], axis=0, keepdims=True))
        else:
            m_new = jnp.maximum(m, mx[...])
        alpha = jnp.exp2(m - m_new)
        m_ref[:, lo:] = m_new
        l = alpha * l_ref[:, lo:]
        acc = alpha * acc_ref[:, lo:]
        for c in range(tk // kc):
            p = jnp.exp2(src[c * kc:(c + 1) * kc, lo:] - m_new)
            l = l + jnp.sum(p, axis=0, keepdims=True)
            acc = acc + _dot(vt_ref[0, 0, :, pl.ds(k0 + c * kc, kc)], p.astype(BF16))
        l_ref[:, lo:] = l
        acc_ref[:, lo:] = acc

    m_ref[...] = jnp.full(m_ref.shape, -jnp.inf, F32)
    l_ref[...] = jnp.zeros(l_ref.shape, F32)
    acc_ref[...] = jnp.zeros(acc_ref.shape, F32)
    put_scores(0, sa_ref, xa_ref)

    def pair(jj, carry):
        j = 2 * jj
        put_scores(j + 1, sb_ref, xb_ref)
        update(j, sa_ref, xa_ref)
        put_scores(j + 2, sa_ref, xa_ref)
        update(j + 1, sb_ref, xb_ref)
        return carry

    lax.fori_loop(0, i, pair, 0)
    put_scores(2 * i + 1, sb_ref, lo=tk)
    update(2 * i, sa_ref, diagonal=True)
    update(2 * i + 1, sb_ref, lo=tk, diagonal=True)
    o_ref[0] = (acc_ref[...] / l_ref[...]).T.astype(o_ref.dtype)


def _attn_prompt(qa, ka, vt):
    b, s, _ = qa.shape
    tq = min(ATT_TILE, s)
    tk = tq // 2
    assert s % tq == 0 and tk % LANES == 0
    return pl.pallas_call(
        _attn_prompt_kernel,
        grid=(b, N_HEADS, s // tq),
        in_specs=[pl.BlockSpec((1, tq, AUG_DIM), lambda b_, h, i: (b_, i, h)),
                  pl.BlockSpec((1, s, AUG_DIM), lambda b_, h, i: (b_, 0, h)),
                  pl.BlockSpec((1, 1, HEAD_DIM, s), lambda b_, h, i: (b_, h, 0, 0))],
        out_specs=pl.BlockSpec((1, tq, HEAD_DIM), lambda b_, h, i: (b_, i, h)),
        out_shape=jax.ShapeDtypeStruct((b, s, D_MODEL), BF16),
        scratch_shapes=[pltpu.VMEM((tk, tq), F32), pltpu.VMEM((tk, tq), F32), pltpu.VMEM((1, tq), F32),
                        pltpu.VMEM((1, tq), F32), pltpu.VMEM((1, tq), F32), pltpu.VMEM((1, tq), F32),
                        pltpu.VMEM((HEAD_DIM, tq), F32)],
        compiler_params=pltpu.CompilerParams(dimension_semantics=("arbitrary", "arbitrary", "arbitrary"),
                                             vmem_limit_bytes=VMEM_LIMIT),
        name="attn_prompt",
    )(qa, ka, vt)


KEYS_PER_GROUP = LANES // N_HEADS


def _attn_sample_kernel(pt_ref, *refs, n_pb, n_grp):
    k_refs = refs[0:n_pb]
    v_refs = refs[n_pb:2 * n_pb]
    f_refs = refs[2 * n_pb:3 * n_pb]
    (q_ref, kn_ref, vn_ref, cq_ref, ckn_ref, uin_ref, usum_ref, o_ref,
     qall_ref, mask_ref, m_ref, l_ref, acc_ref, tail_ref) = refs[3 * n_pb:]
    del pt_ref
    t_new = q_ref.shape[1]
    nq = N_HEADS * t_new
    page_w = n_grp * LANES
    p_idx = pl.program_id(1)

    def online(s, values):
        m_old = m_ref[...]
        m_new = jnp.maximum(m_old, jnp.max(s, axis=-1, keepdims=True))
        alpha = jnp.exp2(m_old - m_new)
        p = jnp.exp2(s - m_new)
        m_ref[...] = m_new
        l_ref[...] = alpha * l_ref[...] + jnp.sum(p, axis=-1, keepdims=True)
        pb = p.astype(BF16)
        pv = None
        for (lo, hi), xv in values:
            term = _dot(pb[:, lo:hi], xv)
            pv = term if pv is None else pv + term
        acc_ref[...] = alpha * acc_ref[...] + pv

    @pl.when(p_idx == 0)
    def _():
        row = lax.broadcasted_iota(jnp.int32, (nq, LANES), 0)
        lane = lax.broadcasted_iota(jnp.int32, (nq, LANES), 1)
        q = q_ref[0]
        qall_ref[...] = jnp.concatenate([q[:, h * HEAD_DIM:(h + 1) * HEAD_DIM] for h in range(N_HEADS)],
                                        axis=0).astype(BF16)
        same_head = (lane & (N_HEADS - 1)) == (row >> _log2(t_new))
        mask_ref[...] = jnp.where(same_head, cq_ref[0] * LOG2E, -jnp.inf)
        m_ref[...] = jnp.full(m_ref.shape, -jnp.inf, F32)
        l_ref[...] = jnp.zeros(l_ref.shape, F32)
        acc_ref[...] = jnp.zeros(acc_ref.shape, F32)
        tail_ref[...] = jnp.zeros(tail_ref.shape, F32)
        pad = LANES - nq
        kn = jnp.concatenate([kn_ref[0], jnp.zeros((pad, HEAD_DIM), F32)], axis=0).astype(BF16)
        vn = jnp.concatenate([vn_ref[0], jnp.zeros((pad, HEAD_DIM), F32)], axis=0).astype(BF16)
        s = _dot_nt(qall_ref[...], kn) + (mask_ref[...] - ckn_ref[0] * LOG2E)
        s = jnp.where((lane >> _log2(N_HEADS)) <= (row & (t_new - 1)), s, -jnp.inf)
        online(s, [((0, LANES), vn)])

    qall = qall_ref[...]
    mask = mask_ref[...]
    tail = tail_ref[...]
    zeros8 = jnp.zeros((SUBLANES, LANES), F32)
    grp_row = lax.broadcasted_iota(jnp.int32, (n_grp, LANES), 0)
    s_parts = []
    for u in range(n_pb):
        lp = f_refs[u][0]
        hi, mid, lo = _split3(lp)
        l3 = jnp.concatenate([hi, mid, lo, zeros8], axis=0).astype(BF16)
        pin = _dot(l3, uin_ref[...])
        psum = _dot(l3, usum_ref[...])
        intra = pin[0:n_grp] + pin[n_grp:2 * n_grp] + pin[2 * n_grp:3 * n_grp]
        tot = psum[0:n_grp] + psum[n_grp:2 * n_grp] + psum[2 * n_grp:3 * n_grp]
        r = tot
        d = 1
        while d < n_grp:
            r = r + jnp.where(grp_row + d < n_grp, pltpu.roll(r, n_grp - d, 0), 0.0)
            d *= 2
        suf = (intra + (r - tot) + tail) * LOG2E
        tail = tail + r[0:1, :]
        sk = _dot_nt(qall, k_refs[u][0].astype(BF16))
        s_parts += [sk[:, g * LANES:(g + 1) * LANES] + (mask + suf[g:g + 1, :]) for g in range(n_grp)]
    tail_ref[...] = tail
    online(jnp.concatenate(s_parts, axis=1),
           [((u * page_w, (u + 1) * page_w), v_refs[u][0].astype(BF16)) for u in range(n_pb)])

    @pl.when(p_idx == pl.num_programs(1) - 1)
    def _():
        o = acc_ref[...] / l_ref[...]
        for h in range(N_HEADS):
            o_ref[0, :, h * HEAD_DIM:(h + 1) * HEAD_DIM] = o[h * t_new:(h + 1) * t_new, :]


def _suffix_matrices():
    idx = np.arange(LANES)
    key, head = idx // N_HEADS, idx % N_HEADS
    same = head[:, None] == head[None, :]
    later = same & (key[:, None] > key[None, :])
    return jnp.asarray(later, BF16), jnp.asarray(same, BF16)


def _attn_sample(page_ids, cache_k, cache_v, cache_logf, q, kn, vn, cq, ckn):
    db, n_pages = page_ids.shape
    page_rows = cache_k.shape[1]
    n_grp = page_rows // LANES
    t_new = q.shape[1]
    nq = N_HEADS * t_new
    assert t_new == SUBLANES and n_grp == SUBLANES and cache_logf.shape[1:] == (n_grp, LANES)
    n_pb = PAGES_PER_STEP if n_pages % PAGES_PER_STEP == 0 else 1
    later, same = _suffix_matrices()

    def page_map(u):
        def index_map(b, p, pt):
            return (pt[b * n_pages + (n_pages - 1 - (p * n_pb + u))], 0, 0)
        return index_map

    per_b = lambda r, w: pl.BlockSpec((1, r, w), lambda b, p, pt: (b, 0, 0))
    const = pl.BlockSpec((LANES, LANES), lambda b, p, pt: (0, 0))
    in_specs = ([pl.BlockSpec((1, page_rows, HEAD_DIM), page_map(u)) for u in range(n_pb)]
                + [pl.BlockSpec((1, page_rows, HEAD_DIM), page_map(u)) for u in range(n_pb)]
                + [pl.BlockSpec((1, n_grp, LANES), page_map(u)) for u in range(n_pb)]
                + [per_b(t_new, D_MODEL), per_b(nq, HEAD_DIM), per_b(nq, HEAD_DIM), per_b(nq, 1), per_b(1, LANES),
                   const, const])
    grid_spec = pltpu.PrefetchScalarGridSpec(
        num_scalar_prefetch=1,
        grid=(db, n_pages // n_pb),
        in_specs=in_specs,
        out_specs=pl.BlockSpec((1, t_new, D_MODEL), lambda b, p, pt: (b, 0, 0)),
        scratch_shapes=[pltpu.VMEM((nq, HEAD_DIM), BF16), pltpu.VMEM((nq, LANES), F32), pltpu.VMEM((nq, 1), F32),
                        pltpu.VMEM((nq, 1), F32), pltpu.VMEM((nq, HEAD_DIM), F32), pltpu.VMEM((1, LANES), F32)],
    )
    return pl.pallas_call(
        functools.partial(_attn_sample_kernel, n_pb=n_pb, n_grp=n_grp),
        grid_spec=grid_spec,
        out_shape=jax.ShapeDtypeStruct((db, t_new, D_MODEL), F32),
        compiler_params=pltpu.CompilerParams(dimension_semantics=("arbitrary", "arbitrary"),
                                             vmem_limit_bytes=VMEM_LIMIT),
        name="attn_sample",
    )(page_ids.reshape(-1), *([cache_k] * n_pb), *([cache_v] * n_pb), *([cache_logf] * n_pb),
      q, kn, vn, cq, ckn, later, same)


def _gelu_tanh(x):
    return 0.5 * x * (1.0 + jnp.tanh(np.float32(np.sqrt(2.0 / np.pi)) * (x + 0.044715 * (x * x * x))))


def _post_kernel(x_ref, og_ref, oa_ref, sga_ref, p_ref, hist_ref,
                 wo_ref, gpost_ref, gpre_ref, wup_ref, fcw_ref, fcb_ref, wdn_ref, gfpost_ref,
                 wple_ref, wpg_ref, gple_ref,
                 y_ref, hout_ref,
                 hh_ref, xg_ref, xv_ref, *, inter, tiles_per_seq):
    tm = x_ref.shape[0]
    nh = (FFN_CONV_W - 1) * inter
    hp = hh_ref.shape[0]
    i = pl.program_id(0)

    @pl.when(i % tiles_per_seq == 0)
    def _():
        if hp > nh:
            hh_ref[...] = jnp.zeros_like(hh_ref)
        hh_ref[hp - nh:hp, :] = hist_ref[0]

    merged = og_ref[...].astype(F32) + sga_ref[...].astype(F32) * oa_ref[...].astype(F32)
    x1 = x_ref[...] + _rms(_dot(merged.astype(BF16), wo_ref[...]), gpost_ref[...])
    hn = _rms(x1, gpre_ref[...]).astype(BF16)

    n_chunks = D_FF // FFN_CHUNK

    def up_conv(c):
        halves = []
        for xp_ref, base in ((xg_ref, 0), (xv_ref, D_FF)):
            cols = slice(base + c * FFN_CHUNK, base + (c + 1) * FFN_CHUNK)
            xp_ref[c % 2, 0:hp, :] = hh_ref[:, cols]
            xp_ref[c % 2, hp:hp + tm, :] = _dot(hn, wup_ref[:, cols])
            hc = fcb_ref[:, cols]
            for j in range(FFN_CONV_W):
                off = hp - (FFN_CONV_W - 1 - j) * inter
                hc = hc + fcw_ref[j:j + 1, cols] * xp_ref[c % 2, off:off + tm, :]
            hh_ref[:, cols] = xp_ref[c % 2, tm:tm + hp, :]
            halves.append(hc)
        return halves

    y2 = jnp.zeros((tm, D_MODEL), F32)
    nxt = up_conv(0)
    for c in range(n_chunks):
        cur = nxt
        if c + 1 < n_chunks:
            nxt = up_conv(c + 1)
        act = (_gelu_tanh(cur[0]) * cur[1]).astype(BF16)
        y2 = y2 + _dot(act, wdn_ref[c * FFN_CHUNK:(c + 1) * FFN_CHUNK, :])
    hout_ref[0] = hh_ref[hp - nh:hp, :]

    x2 = x1 + _rms(y2, gfpost_ref[...])
    e = _dot(p_ref[...].astype(BF16), wple_ref[...]) * _sigmoid(_dot(x2.astype(BF16), wpg_ref[...]))
    y_ref[...] = x2 + _rms(e, gple_ref[...])


def _post(x2, og, oa, sga, p2, hist, inter, seq_rows, wts):
    n = x2.shape[0]
    tm = min(ROW_TILE, seq_rows)
    tiles_per_seq = seq_rows // tm
    assert tiles_per_seq == 1 or inter == 1
    n_groups = n // seq_rows
    nh = (FFN_CONV_W - 1) * inter
    hp = _round_up(nh, SUBLANES)
    assert tm >= hp
    d_ple = p2.shape[1]
    row = lambda w: pl.BlockSpec((tm, w), lambda i: (i, 0))
    grp = pl.BlockSpec((1, nh, 2 * D_FF), lambda i: (i // tiles_per_seq, 0, 0))
    vec = _const_spec((1, D_MODEL))
    return pl.pallas_call(
        functools.partial(_post_kernel, inter=inter, tiles_per_seq=tiles_per_seq),
        grid=(n // tm,),
        in_specs=[row(D_MODEL), row(D_MODEL), row(D_MODEL), row(D_MODEL), row(d_ple), grp,
                  _const_spec((D_MODEL, D_MODEL)), vec, vec, _const_spec((D_MODEL, 2 * D_FF)),
                  _const_spec((FFN_CONV_W, 2 * D_FF)), _const_spec((1, 2 * D_FF)), _const_spec((D_FF, D_MODEL)), vec,
                  _const_spec((d_ple, D_MODEL)), _const_spec((D_MODEL, D_MODEL)), vec],
        out_specs=(row(D_MODEL), grp),
        out_shape=(jax.ShapeDtypeStruct((n, D_MODEL), F32),
                   jax.ShapeDtypeStruct((n_groups, nh, 2 * D_FF), F32)),
        scratch_shapes=[pltpu.VMEM((hp, 2 * D_FF), F32), pltpu.VMEM((2, hp + tm, FFN_CHUNK), F32),
                        pltpu.VMEM((2, hp + tm, FFN_CHUNK), F32)],
        compiler_params=pltpu.CompilerParams(dimension_semantics=("arbitrary",), vmem_limit_bytes=VMEM_LIMIT),
        name="post_ffn",
    )(x2, og, oa, sga, p2, hist, wts["w_out"], wts["g_mix_post"], wts["g_ffn_pre"], wts["w_up"], wts["ffn_cw"],
      wts["ffn_cb"], wts["w_down"], wts["g_ffn_post"], wts["w_ple"], wts["w_ple_gate"], wts["g_ple_post"])


def _prep_weights(l, norm_mix_pre, w_in, b_forget, lru_conv_w, lru_conv_b, lru_wa, lru_ba, lru_wx, lru_bx,
                  lru_lambda, w_out, norm_mix_post, norm_ffn_pre, w_up, ffn_conv_w, ffn_conv_b, w_down,
                  norm_ffn_post, w_ple, w_ple_gate, norm_ple_post):
    d = D_MODEL
    wi = w_in[l]
    offs = np.cumsum([0, d, d, d, d, N_HEADS, d, d])
    piece = lambda k: wi[:, int(offs[k]):int(offs[k + 1])].astype(BF16)
    vec = lambda a: a[l].reshape(1, -1).astype(F32)
    per_tile = MXU_DIM // LRU_BLOCK
    n_tiles = LRU_HEADS // per_tile

    def block_diag(w):
        w4 = w.reshape(n_tiles, per_tile, LRU_BLOCK, LRU_BLOCK)
        eye = jnp.eye(per_tile, dtype=w.dtype)
        return jnp.einsum("thij,hg->thigj", w4, eye).reshape(n_tiles, MXU_DIM, MXU_DIM)

    return {
        "g_mix_pre": vec(norm_mix_pre),
        "w_lru": piece(0), "w_q": piece(1), "w_k": piece(2), "w_v": piece(3),
        "w_f": jnp.pad(piece(4), ((0, 0), (0, LANES - N_HEADS))),
        "w_gl": piece(5), "w_ga": piece(6),
        "b_f": jnp.pad(vec(b_forget), ((0, 0), (0, LANES - N_HEADS))),
        "lru_cw": lru_conv_w[l].astype(F32), "lru_cb": vec(lru_conv_b),
        "lru_wab": jnp.concatenate([block_diag(lru_wa[l]), block_diag(lru_wx[l])], axis=-1).astype(BF16),
        "lru_ba": vec(lru_ba), "lru_bx": vec(lru_bx), "lru_lam": vec(lru_lambda),
        "w_out": w_out[l].astype(BF16), "g_mix_post": vec(norm_mix_post), "g_ffn_pre": vec(norm_ffn_pre),
        "w_up": w_up[l].astype(BF16), "ffn_cw": ffn_conv_w[l].astype(F32), "ffn_cb": vec(ffn_conv_b),
        "w_down": w_down[l].astype(BF16), "g_ffn_post": vec(norm_ffn_post),
        "w_ple": w_ple[l].astype(BF16), "w_ple_gate": w_ple_gate[l].astype(BF16), "g_ple_post": vec(norm_ple_post),
    }


def _prompt_layer(x, p_l, wts):
    b, s, d = x.shape
    n = b * s
    qa, ka, kf, vf, vt, lf, _, sga, og, h_last, lru_hist = _in_proj(
        x.reshape(n, d), jnp.zeros((b, LRU_CONV_W - 1, d), F32), jnp.zeros((b, 1, d), F32), 1, s, wts)
    oa = _attn_prompt(qa.reshape(b, s, -1), ka.reshape(b, s, -1), vt.reshape(b, N_HEADS, HEAD_DIM, s))
    y, ffn_hist = _post(x.reshape(n, d), og, oa.reshape(n, d), sga, p_l.reshape(n, -1),
                        jnp.zeros((b, FFN_CONV_W - 1, 2 * D_FF), F32), 1, s, wts)
    state = (kf.reshape(b, s, N_HEADS, HEAD_DIM), vf.reshape(b, s, N_HEADS, HEAD_DIM),
             lf[:, :N_HEADS].reshape(b, s, N_HEADS), h_last.reshape(b, d), lru_hist, ffn_hist)
    return y.reshape(b, s, d), state


def _sample_layer(x, p_l, cache_k, cache_v, cache_logf, page_ids, lru_buf, lru_h0, ffn_buf, wts):
    db, t, d = x.shape
    n = db * t
    tmaj = lambda a: jnp.swapaxes(a, 0, 1).reshape(a.shape[0] * a.shape[1], -1)
    bmaj = lambda a: jnp.swapaxes(a.reshape(-1, db, a.shape[-1]), 0, 1)

    xt = tmaj(x)
    qa, _, kf, vf, _, lf, c, sga, og, h_last, lru_hist = _in_proj(xt, tmaj(lru_buf)[None], lru_h0[None], db, n, wts)

    k_b, v_b, c_b = bmaj(kf.reshape(n, d)), bmaj(vf.reshape(n, d)), bmaj(c)[..., :N_HEADS]
    q_b = bmaj(qa).astype(F32).reshape(db, t, N_HEADS, AUG_DIM)[..., :HEAD_DIM].reshape(db, t, d)
    cq = jnp.swapaxes(c_b, 1, 2).reshape(db, N_HEADS * t, 1)
    ckn = jnp.pad(c_b.reshape(db, 1, t * N_HEADS), ((0, 0), (0, 0), (0, LANES - t * N_HEADS)))
    oa_b = _attn_sample(page_ids, cache_k, cache_v, cache_logf, q_b,
                        k_b.reshape(db, t * N_HEADS, HEAD_DIM), v_b.reshape(db, t * N_HEADS, HEAD_DIM), cq, ckn)

    y, ffn_hist = _post(xt, og, tmaj(oa_b), sga, tmaj(p_l), tmaj(ffn_buf)[None], db, n, wts)
    state = (k_b.reshape(db, t, N_HEADS, HEAD_DIM), v_b.reshape(db, t, N_HEADS, HEAD_DIM),
             bmaj(lf)[..., :N_HEADS], h_last.reshape(db, d),
             bmaj(lru_hist.reshape(-1, d)), bmaj(ffn_hist.reshape(-1, 2 * D_FF)))
    return bmaj(y), state


def kernel(x_prompt, x_sample, cache_k, cache_v, cache_logf, state_lru_h, state_lru_conv, state_ffn_conv, page_table, p_prompt, p_sample, norm_mix_pre, w_in, b_forget, lru_conv_w, lru_conv_b, lru_wa, lru_ba, lru_wx, lru_bx, lru_lambda, w_out, norm_mix_post, norm_ffn_pre, w_up, ffn_conv_w, ffn_conv_b, w_down, norm_ffn_post, w_ple, w_ple_gate, norm_ple_post):
    depth, n_pool, page = cache_k.shape[:3]
    ck = cache_k.reshape(depth * n_pool, page * N_HEADS, HEAD_DIM)
    cv = cache_v.reshape(depth * n_pool, page * N_HEADS, HEAD_DIM)
    cf = cache_logf.reshape(depth * n_pool, page * N_HEADS // LANES, LANES)
    y_p, y_s = x_prompt, x_sample
    outs_p, outs_s = [], []
    for l in range(depth):
        wts = _prep_weights(l, norm_mix_pre, w_in, b_forget, lru_conv_w, lru_conv_b, lru_wa, lru_ba, lru_wx, lru_bx,
                            lru_lambda, w_out, norm_mix_post, norm_ffn_pre, w_up, ffn_conv_w, ffn_conv_b, w_down,
                            norm_ffn_post, w_ple, w_ple_gate, norm_ple_post)
        y_p, st_p = _prompt_layer(y_p, p_prompt[l], wts)
        y_s, st_s = _sample_layer(y_s, p_sample[l], ck, cv, cf, page_table + l * n_pool,
                                  state_lru_conv[l], state_lru_h[l], state_ffn_conv[l], wts)
        outs_p.append(st_p)
        outs_s.append(st_s)
    stack = lambda outs, k: jnp.stack([o[k] for o in outs])
    res = [y_p, y_s]
    for k in range(6):
        res.append(stack(outs_p, k))
        res.append(stack(outs_s, k))
    return tuple(res)
```
